```python
import jax, jax.numpy as jnp
from jax import lax
import numpy as np

D_MODEL = 1024
BATCH = 8
SEQ = 2048
DEPTH = 4
DEC_BATCH = 128
DEC_SEQ = 8
PAST_LEN = 16384
PAGE_SIZE = 128

D_RNN = D_MODEL
N_RNN_BLOCKS = 16
RNN_BLOCK = D_RNN // N_RNN_BLOCKS
CONV_A_WIDTH = 4
C_LRU = 8.0
D_SC = D_MODEL
CONV_B_WIDTH = 3
PROJ_COLS = D_RNN + 3 * D_SC + 2 * D_MODEL
N_KEYS = 128
N_EXPERTS = N_KEYS * N_KEYS
PEER_HEADS = 8
D_KEY = 256
D_KEY_HALF = D_KEY // 2
TOPK = 16
PEER_BLOCK = 128
EPS = 1e-6

kernel_name = "hybrid_rglru_shortconv_peer_step"


def rmsnorm(x, g):
    xf = x.astype(jnp.float32)
    var = jnp.mean(xf * xf, axis=-1, keepdims=True)
    return (xf * lax.rsqrt(var + EPS)).astype(x.dtype) * g


def modulate(h, shift, scale):
    return h * (1.0 + scale[:, None, :]) + shift[:, None, :]


def causal_depthwise_conv(x, buf, w, b):
    width = w.shape[0]
    t = x.shape[1]
    xp = jnp.concatenate([buf.astype(x.dtype), x], axis=1)
    y = xp[:, 0:t] * w[0]
    for k in range(1, width):
        y = y + xp[:, k:k + t] * w[k]
    if b is not None:
        y = y + b
    return y, xp[:, -(width - 1):]


def block_diag_linear(x, w, b):
    bsz, t, c = x.shape
    xb = x.reshape(bsz, t, N_RNN_BLOCKS, RNN_BLOCK)
    y = jnp.einsum('btnj,njk->btnk', xb, w.astype(jnp.float32))
    return y.reshape(bsz, t, c) + b.astype(jnp.float32)


def rglru(xc, h0, is_prompt, w_a, b_a, w_x, b_x, lam):
    xf = xc.astype(jnp.float32)
    r = jax.nn.sigmoid(block_diag_linear(xf, w_a, b_a))
    i = jax.nn.sigmoid(block_diag_linear(xf, w_x, b_x))
    log_a = -C_LRU * r * jax.nn.softplus(-lam.astype(jnp.float32))
    a = jnp.exp(log_a)
    mult = jnp.sqrt(-jnp.expm1(2.0 * log_a))
    t = xc.shape[1]
    if is_prompt:
        first = (jnp.arange(t) == 0)[None, :, None]
        mult = jnp.where(first, 1.0, mult)
    bterm = mult * (i * xf)

    def combine(left, right):
        a1, b1 = left
        a2, b2 = right
        return a1 * a2, a2 * b1 + b2

    a_cum, b_cum = lax.associative_scan(combine, (a, bterm), axis=1)
    h = a_cum * h0.astype(jnp.float32)[:, None, :] + b_cum
    return h.astype(xc.dtype), h[:, -1].astype(xc.dtype)


def peer(h, w_q, sub_keys, u_tab, v_tab):
    bsz, t, d = h.shape
    hf = h.reshape(-1, d)
    n = hf.shape[0]
    pad = (-n) % PEER_BLOCK
    hp = jnp.pad(hf, ((0, pad), (0, 0))).reshape(-1, PEER_BLOCK, d)

    def block(xb):
        tb = xb.shape[0]
        q = (xb @ w_q).reshape(tb, PEER_HEADS, 2, D_KEY_HALF)
        s = jnp.einsum('thpd,hpkd->thpk', q, sub_keys).astype(jnp.float32)
        sv, si = lax.top_k(s, TOPK)
        cand = sv[:, :, 0, :, None] + sv[:, :, 1, None, :]
        cand_idx = si[:, :, 0, :, None] * N_KEYS + si[:, :, 1, None, :]
        cand = cand.reshape(tb, PEER_HEADS, TOPK * TOPK)
        cand_idx = cand_idx.reshape(tb, PEER_HEADS, TOPK * TOPK)
        fv, fi = lax.top_k(cand, TOPK)
        idx = jnp.take_along_axis(cand_idx, fi, axis=-1)
        g = jax.nn.softmax(fv, axis=-1)
        ue = u_tab[idx]
        act = jax.nn.gelu(jnp.einsum('thkd,td->thk', ue, xb).astype(jnp.float32), approximate=False)
        ve = v_tab[idx]
        return jnp.einsum('thk,thkd->td', (g * act).astype(xb.dtype), ve)

    out = lax.map(block, hp)
    return out.reshape(-1, d)[:n].reshape(bsz, t, d)


def trunk_layer(x, c, buf_a, h0, buf_b, is_prompt, w_ada, b_ada, norm1, norm2, w_in,
                conv_a_w, conv_a_b, rg_w_a, rg_b_a, rg_w_x, rg_b_x, rg_lambda,
                conv_b_w, w_out, peer_w_q, peer_sub_keys, peer_u, peer_v):
    mod = c @ w_ada + b_ada
    sh1, sc1, g1, sh2, sc2, g2 = jnp.split(mod, 6, axis=-1)
    hn = modulate(rmsnorm(x, norm1), sh1, sc1)
    proj = hn @ w_in
    xa = proj[..., :D_RNN]
    o = D_RNN
    bg = proj[..., o:o + D_SC]
    cg = proj[..., o + D_SC:o + 2 * D_SC]
    xs = proj[..., o + 2 * D_SC:o + 3 * D_SC]
    o = o + 3 * D_SC
    ga = proj[..., o:o + D_MODEL]
    gb = proj[..., o + D_MODEL:o + 2 * D_MODEL]
    xa_c, new_buf_a = causal_depthwise_conv(xa, buf_a, conv_a_w, conv_a_b)
    ya, h_last = rglru(xa_c, h0, is_prompt, rg_w_a, rg_b_a, rg_w_x, rg_b_x, rg_lambda)
    u = cg * xs
    uc, new_buf_b = causal_depthwise_conv(u, buf_b, conv_b_w, None)
    yb = bg * uc
    merged = jax.nn.sigmoid(ga) * ya + jax.nn.sigmoid(gb) * yb
    x = x + g1[:, None, :] * (merged @ w_out)
    hn2 = modulate(rmsnorm(x, norm2), sh2, sc2)
    x = x + g2[:, None, :] * peer(hn2, peer_w_q, peer_sub_keys, peer_u, peer_v)
    return x, new_buf_a, h_last, new_buf_b


def setup_inputs(seed: int = 0) -> dict:
    key = jax.random.key(seed)
    ks = jax.random.split(key, 32)
    D = D_MODEL
    f32 = jnp.float32

    def nrm(k, shape, s):
        return jax.random.normal(k, shape, f32) * s

    a0 = jax.random.uniform(ks[18], (DEPTH, D_RNN), f32, 0.9, 0.999)
    a_base = a0 ** (1.0 / C_LRU)
    rg_lambda = jnp.log(a_base) - jnp.log1p(-a_base)
    return {
        'x_prompt': nrm(ks[0], (BATCH, SEQ, D), 1.0),
        'x_sample': nrm(ks[1], (DEC_BATCH, DEC_SEQ, D), 1.0),
        'c_prompt': nrm(ks[2], (BATCH, D), 1.0),
        'c_sample': nrm(ks[3], (DEC_BATCH, D), 1.0),
        'state_conv_a': nrm(ks[4], (DEPTH, DEC_BATCH, CONV_A_WIDTH - 1, D_RNN), 1.0),
        'state_h': nrm(ks[5], (DEPTH, DEC_BATCH, D_RNN), 0.5),
        'state_conv_b': nrm(ks[6], (DEPTH, DEC_BATCH, CONV_B_WIDTH - 1, D_SC), 1.0),
        'w_ada': nrm(ks[7], (DEPTH, D, 6 * D), 0.5 * D ** -0.5),
        'b_ada': nrm(ks[8], (DEPTH, 6 * D), 0.02),
        'norm1': 1.0 + nrm(ks[9], (DEPTH, D), 0.02),
        'norm2': 1.0 + nrm(ks[10], (DEPTH, D), 0.02),
        'w_in': nrm(ks[11], (DEPTH, D, PROJ_COLS), D ** -0.5),
        'conv_a_w': nrm(ks[12], (DEPTH, CONV_A_WIDTH, D_RNN), CONV_A_WIDTH ** -0.5),
        'conv_a_b': nrm(ks[13], (DEPTH, D_RNN), 0.02),
        'rg_w_a': nrm(ks[14], (DEPTH, N_RNN_BLOCKS, RNN_BLOCK, RNN_BLOCK), RNN_BLOCK ** -0.5),
        'rg_b_a': nrm(ks[15], (DEPTH, D_RNN), 0.02),
        'rg_w_x': nrm(ks[16], (DEPTH, N_RNN_BLOCKS, RNN_BLOCK, RNN_BLOCK), RNN_BLOCK ** -0.5),
        'rg_b_x': nrm(ks[17], (DEPTH, D_RNN), 0.02),
        'rg_lambda': rg_lambda,
        'conv_b_w': nrm(ks[19], (DEPTH, CONV_B_WIDTH, D_SC), CONV_B_WIDTH ** -0.5),
        'w_out': nrm(ks[20], (DEPTH, D, D), D ** -0.5),
        'peer_w_q': nrm(ks[21], (DEPTH, D, PEER_HEADS * D_KEY), D ** -0.5),
        'peer_sub_keys': nrm(ks[22], (DEPTH, PEER_HEADS, 2, N_KEYS, D_KEY_HALF), D_KEY_HALF ** -0.5),
        'peer_u': nrm(ks[23], (DEPTH, N_EXPERTS, D), D ** -0.5),
        'peer_v': nrm(ks[24], (DEPTH, N_EXPERTS, D), 0.5 * PEER_HEADS ** -0.5),
        'final_norm': 1.0 + nrm(ks[25], (D,), 0.02),
    }


def reference(x_prompt, x_sample, c_prompt, c_sample, state_conv_a, state_h, state_conv_b,
              w_ada, b_ada, norm1, norm2, w_in, conv_a_w, conv_a_b, rg_w_a, rg_b_a,
              rg_w_x, rg_b_x, rg_lambda, conv_b_w, w_out, peer_w_q, peer_sub_keys,
              peer_u, peer_v, final_norm):
    xp = x_prompt
    xs = x_sample
    bp = x_prompt.shape[0]
    buf_a_p = jnp.zeros((bp, CONV_A_WIDTH - 1, D_RNN), x_prompt.dtype)
    buf_b_p = jnp.zeros((bp, CONV_B_WIDTH - 1, D_SC), x_prompt.dtype)
    h0_p = jnp.zeros((bp, D_RNN), x_prompt.dtype)
    ca_p, hh_p, cb_p, ca_s, hh_s, cb_s = [], [], [], [], [], []
    for l in range(DEPTH):
        w = (w_ada[l], b_ada[l], norm1[l], norm2[l], w_in[l], conv_a_w[l], conv_a_b[l],
             rg_w_a[l], rg_b_a[l], rg_w_x[l], rg_b_x[l], rg_lambda[l], conv_b_w[l],
             w_out[l], peer_w_q[l], peer_sub_keys[l], peer_u[l], peer_v[l])
        xp, na, nh, nb = trunk_layer(xp, c_prompt, buf_a_p, h0_p, buf_b_p, True, *w)
        ca_p.append(na)
        hh_p.append(nh)
        cb_p.append(nb)
        xs, na, nh, nb = trunk_layer(xs, c_sample, state_conv_a[l], state_h[l],
                                     state_conv_b[l], False, *w)
        ca_s.append(na)
        hh_s.append(nh)
        cb_s.append(nb)
    y_prompt = rmsnorm(xp, final_norm)
    y_sample = rmsnorm(xs, final_norm)
    return (y_prompt, y_sample, jnp.stack(ca_p, 0), jnp.stack(hh_p, 0), jnp.stack(cb_p, 0),
            jnp.stack(ca_s, 0), jnp.stack(hh_s, 0), jnp.stack(cb_s, 0))
```

```python
import functools

import jax
import jax.numpy as jnp
from jax import lax
from jax.experimental import pallas as pl
from jax.experimental.pallas import tpu as pltpu

F32 = jnp.float32
BF16 = jnp.bfloat16

EPS = 1e-6
C_LRU = 8.0
N_RNN_BLOCKS = 16
N_KEYS = 128
PEER_HEADS = 8
TOPK = 16
N_MOD = 6

VMEM_LIMIT_BYTES = 56 * 1024 * 1024
ROW_TILE = 512
ROUTER_ROW_TILE = 256
EXPERT_TILE = 512
MIXER_ROWS = 256


def _params(*sem):
    return pltpu.CompilerParams(dimension_semantics=sem, vmem_limit_bytes=VMEM_LIMIT_BYTES)


def _time_tile(rows, t, b):
    tt = max(1, min(t, rows // b))
    assert t % tt == 0, (t, tt)
    return tt


def _ada_kernel(c_ref, w_ref, b_ref, o_ref):
    c = c_ref[...].astype(BF16)
    w = w_ref[...].astype(BF16)
    o_ref[...] = jnp.dot(c, w, preferred_element_type=F32) + b_ref[...]


def _ada(c_all, w_ada, b_ada):
    depth, d, cols = w_ada.shape
    bc = c_all.shape[0]
    tn = 1536
    assert cols % tn == 0
    return pl.pallas_call(
        _ada_kernel,
        grid=(depth, cols // tn),
        in_specs=[
            pl.BlockSpec((bc, d), lambda l, j: (0, 0)),
            pl.BlockSpec((None, d, tn), lambda l, j: (l, 0, j)),
            pl.BlockSpec((None, 1, tn), lambda l, j: (l, 0, j)),
        ],
        out_specs=pl.BlockSpec((None, bc, tn), lambda l, j: (l, 0, j)),
        out_shape=jax.ShapeDtypeStruct((depth, bc, cols), F32),
        compiler_params=_params("parallel", "parallel"),
        name="ada_mod",
    )(c_all, w_ada, b_ada.reshape(depth, 1, cols))


def _norm_mod(x, g, sh, sc):
    var = jnp.mean(x * x, axis=-1, keepdims=True)
    xn = (x * lax.rsqrt(var + EPS)) * g
    return xn * (1.0 + sc) + sh


def _inproj_kernel(x_ref, g_ref, sh_ref, sc_ref, w_ref, o_ref, hn_ref):
    tt, b, d = x_ref.shape

    @pl.when(pl.program_id(1) == 0)
    def _():
        hn = _norm_mod(x_ref[...], g_ref[...], sh_ref[...], sc_ref[...])
        hn_ref[...] = hn.reshape(tt * b, d).astype(BF16)

    y = jnp.dot(hn_ref[...], w_ref[...], preferred_element_type=F32)
    o_ref[...] = y.reshape(tt, b, -1)


def _inproj(x, g, sh, sc, w_bf16):
    t, b, d = x.shape
    cols = w_bf16.shape[1]
    tt = _time_tile(ROW_TILE, t, b)
    tn = 1536
    return pl.pallas_call(
        _inproj_kernel,
        grid=(t // tt, cols // tn),
        in_specs=[
            pl.BlockSpec((tt, b, d), lambda i, j: (i, 0, 0)),
            pl.BlockSpec((1, d), lambda i, j: (0, 0)),
            pl.BlockSpec((b, d), lambda i, j: (0, 0)),
            pl.BlockSpec((b, d), lambda i, j: (0, 0)),
            pl.BlockSpec((d, tn), lambda i, j: (0, j)),
        ],
        out_specs=pl.BlockSpec((tt, b, tn), lambda i, j: (i, 0, j)),
        out_shape=jax.ShapeDtypeStruct((t, b, cols), F32),
        scratch_shapes=[pltpu.VMEM((tt * b, d), BF16)],
        compiler_params=_params("parallel", "arbitrary"),
        name="in_proj",
    )(x, g, sh, sc, w_bf16)


def _mixer_kernel(is_prompt, proj_ref, x_ref, bufa_ref, h0_ref, bufb_ref, g1_ref,
                  caw_ref, cab_ref, wa_ref, ba_ref, wx_ref, bx_ref, lam_ref, cbw_ref, wout_ref,
                  xo_ref, na_ref, nh_ref, nb_ref,
                  xbuf, ubuf, h_scr, a_scr, b_scr):
    tt, bb, d = x_ref.shape
    wa_hist = bufa_ref.shape[0]
    wb_hist = bufb_ref.shape[0]
    ti = pl.program_id(1)

    @pl.when(ti == 0)
    def _():
        xbuf[0:wa_hist] = bufa_ref[...]
        ubuf[0:wb_hist] = bufb_ref[...]
        h_scr[...] = h0_ref[...]

    xa = proj_ref[:, :, 0:d]
    xbuf[wa_hist:wa_hist + tt] = xa
    xc = xbuf[0:tt] * caw_ref[0:1, :]
    for k in range(1, wa_hist + 1):
        xc = xc + xbuf[k:k + tt] * caw_ref[k:k + 1, :]
    xc = xc + cab_ref[...]
    xc2 = xc.reshape(tt * bb, d)
    xcb = xc2.astype(BF16)
    r = jax.nn.sigmoid(jnp.dot(xcb, wa_ref[...], preferred_element_type=F32) + ba_ref[...])
    ig = jax.nn.sigmoid(jnp.dot(xcb, wx_ref[...], preferred_element_type=F32) + bx_ref[...])
    nl = -lam_ref[...]
    softplus = jnp.maximum(nl, 0.0) + jnp.log1p(jnp.exp(-jnp.abs(nl)))
    log_a = (-C_LRU * r) * softplus
    a = jnp.exp(log_a)
    mult = jnp.sqrt(1.0 - a * a)
    if is_prompt:
        row = lax.broadcasted_iota(jnp.int32, (tt * bb, 1), 0)
        mult = jnp.where((row < bb) & (ti == 0), 1.0, mult)
    a_scr[...] = a.reshape(tt, bb, d)
    b_scr[...] = (mult * (ig * xc2)).reshape(tt, bb, d)

    def step(t, h):
        h = a_scr[t] * h + b_scr[t]
        b_scr[t] = h
        return h

    h = lax.fori_loop(0, tt, step, h_scr[...])
    h_scr[...] = h
    ya = b_scr[...]

    u = proj_ref[:, :, 2 * d:3 * d] * proj_ref[:, :, 3 * d:4 * d]
    ubuf[wb_hist:wb_hist + tt] = u
    uc = ubuf[0:tt] * cbw_ref[0:1, :]
    for k in range(1, wb_hist + 1):
        uc = uc + ubuf[k:k + tt] * cbw_ref[k:k + 1, :]
    yb = proj_ref[:, :, d:2 * d] * uc

    merged = (jax.nn.sigmoid(proj_ref[:, :, 4 * d:5 * d]) * ya
              + jax.nn.sigmoid(proj_ref[:, :, 5 * d:6 * d]) * yb)
    o = jnp.dot(merged.reshape(tt * bb, d).astype(BF16), wout_ref[...],
                preferred_element_type=F32)
    xo_ref[...] = x_ref[...] + g1_ref[...] * o.reshape(tt, bb, d)

    new_a = xbuf[tt:tt + wa_hist]
    new_b = ubuf[tt:tt + wb_hist]
    xbuf[0:wa_hist] = new_a
    ubuf[0:wb_hist] = new_b
    na_ref[...] = new_a
    nb_ref[...] = new_b
    nh_ref[...] = h


def _mixer(is_prompt, proj, x, bufa, h0, bufb, g1, caw, cab, wa, ba, wx, bx, lam, cbw, wout):
    t, b, d = x.shape
    cols = proj.shape[2]
    bb = min(b, 32)
    tt = _time_tile(MIXER_ROWS, t, bb)
    wa_hist, wb_hist = bufa.shape[0], bufb.shape[0]
    row = lambda a: a.reshape(1, d)
    full = lambda shape: pl.BlockSpec(shape, lambda i, j: (0,) * len(shape))
    return pl.pallas_call(
        functools.partial(_mixer_kernel, is_prompt),
        grid=(b // bb, t // tt),
        in_specs=[
            pl.BlockSpec((tt, bb, cols), lambda i, j: (j, i, 0)),
            pl.BlockSpec((tt, bb, d), lambda i, j: (j, i, 0)),
            pl.BlockSpec((wa_hist, bb, d), lambda i, j: (0, i, 0)),
            pl.BlockSpec((bb, d), lambda i, j: (i, 0)),
            pl.BlockSpec((wb_hist, bb, d), lambda i, j: (0, i, 0)),
            pl.BlockSpec((bb, d), lambda i, j: (i, 0)),
            full((wa_hist + 1, d)), full((1, d)),
            full((d, d)), full((1, d)), full((d, d)), full((1, d)), full((1, d)),
            full((wb_hist + 1, d)), full((d, d)),
        ],
        out_specs=[
            pl.BlockSpec((tt, bb, d), lambda i, j: (j, i, 0)),
            pl.BlockSpec((wa_hist, bb, d), lambda i, j: (0, i, 0)),
            pl.BlockSpec((bb, d), lambda i, j: (i, 0)),
            pl.BlockSpec((wb_hist, bb, d), lambda i, j: (0, i, 0)),
        ],
        out_shape=[
            jax.ShapeDtypeStruct((t, b, d), F32),
            jax.ShapeDtypeStruct((wa_hist, b, d), F32),
            jax.ShapeDtypeStruct((b, d), F32),
            jax.ShapeDtypeStruct((wb_hist, b, d), F32),
        ],
        scratch_shapes=[
            pltpu.VMEM((wa_hist + tt, bb, d), F32),
            pltpu.VMEM((wb_hist + tt, bb, d), F32),
            pltpu.VMEM((bb, d), F32),
            pltpu.VMEM((tt, bb, d), F32),
            pltpu.VMEM((tt, bb, d), F32),
        ],
        compiler_params=_params("parallel", "arbitrary"),
        name="mixer",
    )(proj, x, bufa, h0, bufb, g1, caw, row(cab), wa, row(ba), wx, row(bx), row(lam), cbw, wout)


def _topk_rank(s, k):
    n, tl = s.shape
    iota = lax.broadcasted_iota(jnp.int32, (n, tl), 0).astype(F32)
    rank = jnp.full((n, tl), float(k), F32)
    vals = []
    for j in range(k):
        m = jnp.max(s, axis=0, keepdims=True)
        idx = jnp.min(jnp.where(s == m, iota, float(n)), axis=0, keepdims=True)
        sel = iota == idx
        rank = jnp.where(sel, float(j), rank)
        s = jnp.where(sel, -jnp.inf, s)
        vals.append(m)
    return vals, rank


def _router_kernel(x_ref, g_ref, sh_ref, sc_ref, wqt_ref, keys_ref,
                   xt_ref, nf_ref, c_ref, r1_ref, e1_ref, qt_scr):
    tt, b, d = x_ref.shape
    tm = tt * b
    hn = _norm_mod(x_ref[...], g_ref[...], sh_ref[...], sc_ref[...]).reshape(tm, d)
    xt = hn.T.astype(BF16)
    xt_ref[...] = xt
    qt_scr[...] = jnp.dot(wqt_ref[...], xt, preferred_element_type=F32)

    def head(h, carry):
        s, vals, rank = [], [], []
        for p in range(2):
            row0 = pl.multiple_of((2 * h + p) * N_KEYS, N_KEYS)
            qc = qt_scr[pl.ds(row0, N_KEYS), :].astype(BF16)
            sp = jnp.dot(keys_ref[2 * h + p], qc, preferred_element_type=F32)
            v, r = _topk_rank(sp, TOPK)
            s.append(sp)
            vals.append(v)
            rank.append(r)
        sv1 = jnp.concatenate(vals[1], axis=0)
        cand = jnp.concatenate([vals[0][a] + sv1 for a in range(TOPK)], axis=0)
        cvals, crank = _topk_rank(cand, TOPK)
        chosen = crank < float(TOPK)
        z = jnp.sum(jnp.where(chosen, jnp.exp(cand - cvals[0]), 0.0), axis=0, keepdims=True)
        nsel = jnp.zeros((N_KEYS, tm), F32)
        for a in range(TOPK):
            n_a = jnp.sum(jnp.where(chosen[a * TOPK:(a + 1) * TOPK], 1.0, 0.0),
                          axis=0, keepdims=True)
            nsel = nsel + jnp.where(rank[0] == float(a), n_a, 0.0)
        nf_ref[h] = nsel
        c_ref[h] = jnp.exp(s[0] - vals[0][0]) / z
        r1_ref[h] = rank[1]
        e1_ref[h] = jnp.exp(s[1] - vals[1][0])
        return carry

    lax.fori_loop(0, PEER_HEADS, head, 0)


def _router(x, g, sh, sc, wqt_bf16, keys_bf16):
    t, b, d = x.shape
    n = t * b
    tt = _time_tile(ROUTER_ROW_TILE, t, b)
    tm = tt * b
    dq = wqt_bf16.shape[0]
    fac = jax.ShapeDtypeStruct((PEER_HEADS, N_KEYS, n), F32)
    fac_spec = pl.BlockSpec((PEER_HEADS, N_KEYS, tm), lambda i: (0, 0, i))
    return pl.pallas_call(
        _router_kernel,
        grid=(t // tt,),
        in_specs=[
            pl.BlockSpec((tt, b, d), lambda i: (i, 0, 0)),
            pl.BlockSpec((1, d), lambda i: (0, 0)),
            pl.BlockSpec((b, d), lambda i: (0, 0)),
            pl.BlockSpec((b, d), lambda i: (0, 0)),
            pl.BlockSpec((dq, d), lambda i: (0, 0)),
            pl.BlockSpec((2 * PEER_HEADS, N_KEYS, N_KEYS), lambda i: (0, 0, 0)),
        ],
        out_specs=[pl.BlockSpec((d, tm), lambda i: (0, i)), fac_spec, fac_spec, fac_spec, fac_spec],
        out_shape=[jax.ShapeDtypeStruct((d, n), BF16), fac, fac, fac, fac],
        scratch_shapes=[pltpu.VMEM((dq, tm), F32)],
        compiler_params=_params("parallel"),
        name="peer_router",
    )(x, g, sh, sc, wqt_bf16, keys_bf16)


def _peer_kernel(xt_ref, u_ref, vt_ref, nf_ref, c_ref, r1_ref, e1_ref, x_ref, g2_ref,
                 o_ref, acc_ref, s_ref, w_ref):
    tt, b, d = x_ref.shape
    tn = u_ref.shape[0]
    groups = tn // N_KEYS
    j = pl.program_id(1)

    @pl.when(j == 0)
    def _():
        acc_ref[...] = jnp.zeros_like(acc_ref)

    s_ref[...] = jnp.dot(u_ref[...], xt_ref[...], preferred_element_type=F32)
    for ii in range(groups):
        i = j * groups + ii
        gate = None
        for h in range(PEER_HEADS):
            n_row = nf_ref[h, pl.ds(i, 1), :]
            c_row = c_ref[h, pl.ds(i, 1), :]
            term = c_row * jnp.where(r1_ref[h] < n_row, e1_ref[h], 0.0)
            gate = term if gate is None else gate + term
        s = s_ref[ii * N_KEYS:(ii + 1) * N_KEYS, :]
        act = (0.5 * s) * (1.0 + lax.erf(s * (0.5 ** 0.5)))
        w_ref[ii * N_KEYS:(ii + 1) * N_KEYS, :] = (gate * act).astype(BF16)
    acc_ref[...] += jnp.dot(vt_ref[...], w_ref[...], preferred_element_type=F32)

    @pl.when(j == pl.num_programs(1) - 1)
    def _():
        o = acc_ref[...].T.reshape(tt, b, d)
        o_ref[...] = x_ref[...] + g2_ref[...] * o


def _peer(xt, u_bf16, vt_bf16, nf, c, r1, e1, x, g2):
    t, b, d = x.shape
    n = t * b
    ne = u_bf16.shape[0]
    tt = _time_tile(ROW_TILE, t, b)
    tm = tt * b
    tn = EXPERT_TILE
    fac_spec = pl.BlockSpec((PEER_HEADS, N_KEYS, tm), lambda i, j: (0, 0, i))
    return pl.pallas_call(
        _peer_kernel,
        grid=(n // tm, ne // tn),
        in_specs=[
            pl.BlockSpec((d, tm), lambda i, j: (0, i)),
            pl.BlockSpec((tn, d), lambda i, j: (j, 0)),
            pl.BlockSpec((d, tn), lambda i, j: (0, j)),
            fac_spec, fac_spec, fac_spec, fac_spec,
            pl.BlockSpec((tt, b, d), lambda i, j: (i, 0, 0)),
            pl.BlockSpec((b, d), lambda i, j: (0, 0)),
        ],
        out_specs=pl.BlockSpec((tt, b, d), lambda i, j: (i, 0, 0)),
        out_shape=jax.ShapeDtypeStruct((t, b, d), F32),
        scratch_shapes=[
            pltpu.VMEM((d, tm), F32),
            pltpu.VMEM((tn, tm), F32),
            pltpu.VMEM((tn, tm), BF16),
        ],
        compiler_params=_params("parallel", "arbitrary"),
        name="peer_experts",
    )(xt, u_bf16, vt_bf16, nf, c, r1, e1, x, g2)


def _final_norm_kernel(x_ref, g_ref, o_ref):
    x = x_ref[...]
    var = jnp.mean(x * x, axis=-1, keepdims=True)
    o_ref[...] = (x * lax.rsqrt(var + EPS)) * g_ref[...]


def _final_norm(x, g):
    t, b, d = x.shape
    tt = _time_tile(ROW_TILE, t, b)
    return pl.pallas_call(
        _final_norm_kernel,
        grid=(t // tt,),
        in_specs=[pl.BlockSpec((tt, b, d), lambda i: (i, 0, 0)),
                  pl.BlockSpec((1, d), lambda i: (0, 0))],
        out_specs=pl.BlockSpec((tt, b, d), lambda i: (i, 0, 0)),
        out_shape=jax.ShapeDtypeStruct((t, b, d), F32),
        compiler_params=_params("parallel"),
        name="final_norm",
    )(x, g)


def _block_diag(w):
    nb, bs, _ = w.shape
    eye = jnp.eye(nb, dtype=w.dtype)
    return (eye[:, None, :, None] * w[:, :, None, :]).reshape(nb * bs, nb * bs)


def _layer(x, mod, states, is_prompt, lw):
    sh1, sc1, g1, sh2, sc2, g2 = mod
    bufa, h0, bufb = states
    proj = _inproj(x, lw["norm1"], sh1, sc1, lw["w_in"])
    x, na, nh, nb = _mixer(is_prompt, proj, x, bufa, h0, bufb, g1,
                           lw["conv_a_w"], lw["conv_a_b"], lw["rg_w_a"], lw["rg_b_a"],
                           lw["rg_w_x"], lw["rg_b_x"], lw["rg_lambda"], lw["conv_b_w"], lw["w_out"])
    xt, nf, c, r1, e1 = _router(x, lw["norm2"], sh2, sc2, lw["w_qt"], lw["keys"])
    x = _peer(xt, lw["u"], lw["vt"], nf, c, r1, e1, x, g2)
    return x, na, nh, nb


def kernel(x_prompt, x_sample, c_prompt, c_sample, state_conv_a, state_h, state_conv_b, w_ada, b_ada, norm1, norm2, w_in, conv_a_w, conv_a_b, rg_w_a, rg_b_a, rg_w_x, rg_b_x, rg_lambda, conv_b_w, w_out, peer_w_q, peer_sub_keys, peer_u, peer_v, final_norm):
    depth = w_ada.shape[0]
    bp, _, d = x_prompt.shape
    bs = x_sample.shape[0]
    wa_hist = conv_a_w.shape[1] - 1
    wb_hist = conv_b_w.shape[1] - 1

    mods = _ada(jnp.concatenate([c_prompt, c_sample], axis=0), w_ada, b_ada)
    mods = mods.reshape(depth, bp + bs, N_MOD, d)

    xp = jnp.transpose(x_prompt, (1, 0, 2))
    xs = jnp.transpose(x_sample, (1, 0, 2))
    zeros_p = (jnp.zeros((wa_hist, bp, d), F32), jnp.zeros((bp, d), F32),
               jnp.zeros((wb_hist, bp, d), F32))
    outs_p, outs_s = [], []
    for l in range(depth):
        lw = {
            "norm1": norm1[l].reshape(1, d), "norm2": norm2[l].reshape(1, d),
            "w_in": w_in[l].astype(BF16),
            "conv_a_w": conv_a_w[l], "conv_a_b": conv_a_b[l],
            "rg_w_a": _block_diag(rg_w_a[l]).astype(BF16), "rg_b_a": rg_b_a[l],
            "rg_w_x": _block_diag(rg_w_x[l]).astype(BF16), "rg_b_x": rg_b_x[l],
            "rg_lambda": rg_lambda[l], "conv_b_w": conv_b_w[l],
            "w_out": w_out[l].astype(BF16),
            "w_qt": peer_w_q[l].T.astype(BF16),
            "keys": peer_sub_keys[l].reshape(2 * PEER_HEADS, N_KEYS, -1).astype(BF16),
            "u": peer_u[l].astype(BF16),
            "vt": peer_v[l].T.astype(BF16),
        }
        mod_p = [mods[l, :bp, k] for k in range(N_MOD)]
        mod_s = [mods[l, bp:, k] for k in range(N_MOD)]
        xp, na, nh, nb = _layer(xp, mod_p, zeros_p, True, lw)
        outs_p.append((na, nh, nb))
        states_s = (jnp.transpose(state_conv_a[l], (1, 0, 2)), state_h[l],
                    jnp.transpose(state_conv_b[l], (1, 0, 2)))
        xs, na, nh, nb = _layer(xs, mod_s, states_s, False, lw)
        outs_s.append((na, nh, nb))

    fn = final_norm.reshape(1, d)
    y_prompt = jnp.transpose(_final_norm(xp, fn), (1, 0, 2))
    y_sample = jnp.transpose(_final_norm(xs, fn), (1, 0, 2))

    def stack(outs):
        ca = jnp.stack([jnp.transpose(o[0], (1, 0, 2)) for o in outs], 0)
        hh = jnp.stack([o[1] for o in outs], 0)
        cb = jnp.stack([jnp.transpose(o[2], (1, 0, 2)) for o in outs], 0)
        return ca, hh, cb

    ca_p, hh_p, cb_p = stack(outs_p)
    ca_s, hh_s, cb_s = stack(outs_s)
    return (y_prompt, y_sample, ca_p, hh_p, cb_p, ca_s, hh_s, cb_s)
```

```python
import functools

import jax
import jax.numpy as jnp
from jax import lax
from jax.experimental import pallas as pl
from jax.experimental.pallas import tpu as pltpu

F32 = jnp.float32
BF16 = jnp.bfloat16

EPS = 1e-6
C_LRU = 8.0
N_RNN_BLOCKS = 16
N_KEYS = 128
PEER_HEADS = 8
TOPK = 16
N_MOD = 6

VMEM_LIMIT_BYTES = 56 * 1024 * 1024
ROW_TILE = 512
ROUTER_ROW_TILE = 256
EXPERT_TILE = 512
MIXER_ROWS = 256
LANES = 128
SUBLANES = 8
BF16_ROWS = 16
MXU_COLS = 256


def _params(*sem):
    return pltpu.CompilerParams(dimension_semantics=sem, vmem_limit_bytes=VMEM_LIMIT_BYTES)


def _time_tile(rows, t, b):
    tt = max(1, min(t, rows // b))
    assert t % tt == 0, (t, tt)
    return tt


def _ada_kernel(c_ref, w_ref, b_ref, o_ref):
    c = c_ref[...].astype(BF16)
    w = w_ref[...].astype(BF16)
    o_ref[...] = jnp.dot(c, w, preferred_element_type=F32) + b_ref[...]


def _ada(c_all, w_ada, b_ada):
    depth, d, cols = w_ada.shape
    bc = c_all.shape[0]
    tn = 1536
    assert cols % tn == 0
    return pl.pallas_call(
        _ada_kernel,
        grid=(depth, cols // tn),
        in_specs=[
            pl.BlockSpec((bc, d), lambda l, j: (0, 0)),
            pl.BlockSpec((None, d, tn), lambda l, j: (l, 0, j)),
            pl.BlockSpec((None, 1, tn), lambda l, j: (l, 0, j)),
        ],
        out_specs=pl.BlockSpec((None, bc, tn), lambda l, j: (l, 0, j)),
        out_shape=jax.ShapeDtypeStruct((depth, bc, cols), F32),
        compiler_params=_params("parallel", "parallel"),
        name="ada_mod",
    )(c_all, w_ada, b_ada.reshape(depth, 1, cols))


def _norm_mod(x, g, sh, sc):
    var = jnp.mean(x * x, axis=-1, keepdims=True)
    xn = (x * lax.rsqrt(var + EPS)) * g
    return xn * (1.0 + sc) + sh


def _inproj_kernel(x_ref, g_ref, sh_ref, sc_ref, w_ref, o_ref, hn_ref):
    tt, b, d = x_ref.shape

    @pl.when(pl.program_id(1) == 0)
    def _():
        hn = _norm_mod(x_ref[...], g_ref[...], sh_ref[...], sc_ref[...])
        hn_ref[...] = hn.reshape(tt * b, d).astype(BF16)

    y = jnp.dot(hn_ref[...], w_ref[...], preferred_element_type=F32)
    o_ref[...] = y.reshape(tt, b, -1)


def _inproj(x, g, sh, sc, w_bf16):
    t, b, d = x.shape
    cols = w_bf16.shape[1]
    tt = _time_tile(ROW_TILE, t, b)
    tn = 1536
    return pl.pallas_call(
        _inproj_kernel,
        grid=(t // tt, cols // tn),
        in_specs=[
            pl.BlockSpec((tt, b, d), lambda i, j: (i, 0, 0)),
            pl.BlockSpec((1, d), lambda i, j: (0, 0)),
            pl.BlockSpec((b, d), lambda i, j: (0, 0)),
            pl.BlockSpec((b, d), lambda i, j: (0, 0)),
            pl.BlockSpec((d, tn), lambda i, j: (0, j)),
        ],
        out_specs=pl.BlockSpec((tt, b, tn), lambda i, j: (i, 0, j)),
        out_shape=jax.ShapeDtypeStruct((t, b, cols), F32),
        scratch_shapes=[pltpu.VMEM((tt * b, d), BF16)],
        compiler_params=_params("parallel", "arbitrary"),
        name="in_proj",
    )(x, g, sh, sc, w_bf16)


def _mixer_kernel(is_prompt, proj_ref, x_ref, bufa_ref, h0_ref, bufb_ref, g1_ref,
                  caw_ref, cab_ref, wa_ref, ba_ref, wx_ref, bx_ref, lam_ref, cbw_ref, wout_ref,
                  xo_ref, na_ref, nh_ref, nb_ref,
                  xbuf, ubuf, h_scr, a_scr, b_scr):
    tt, bb, d = x_ref.shape
    wa_hist = bufa_ref.shape[0]
    wb_hist = bufb_ref.shape[0]
    ti = pl.program_id(1)

    @pl.when(ti == 0)
    def _():
        xbuf[0:wa_hist] = bufa_ref[...]
        ubuf[0:wb_hist] = bufb_ref[...]
        h_scr[...] = h0_ref[...]

    xa = proj_ref[:, :, 0:d]
    xbuf[wa_hist:wa_hist + tt] = xa
    xc = xbuf[0:tt] * caw_ref[0:1, :]
    for k in range(1, wa_hist + 1):
        xc = xc + xbuf[k:k + tt] * caw_ref[k:k + 1, :]
    xc = xc + cab_ref[...]
    xc2 = xc.reshape(tt * bb, d)
    xcb = xc2.astype(BF16)
    r = jax.nn.sigmoid(jnp.dot(xcb, wa_ref[...], preferred_element_type=F32) + ba_ref[...])
    ig = jax.nn.sigmoid(jnp.dot(xcb, wx_ref[...], preferred_element_type=F32) + bx_ref[...])
    nl = -lam_ref[...]
    softplus = jnp.maximum(nl, 0.0) + jnp.log1p(jnp.exp(-jnp.abs(nl)))
    log_a = (-C_LRU * r) * softplus
    a = jnp.exp(log_a)
    mult = jnp.sqrt(1.0 - a * a)
    if is_prompt:
        row = lax.broadcasted_iota(jnp.int32, (tt * bb, 1), 0)
        mult = jnp.where((row < bb) & (ti == 0), 1.0, mult)
    a_scr[...] = a.reshape(tt, bb, d)
    b_scr[...] = (mult * (ig * xc2)).reshape(tt, bb, d)

    def step(t, h):
        h = a_scr[t] * h + b_scr[t]
        b_scr[t] = h
        return h

    h = lax.fori_loop(0, tt, step, h_scr[...])
    h_scr[...] = h
    ya = b_scr[...]

    u = proj_ref[:, :, 2 * d:3 * d] * proj_ref[:, :, 3 * d:4 * d]
    ubuf[wb_hist:wb_hist + tt] = u
    uc = ubuf[0:tt] * cbw_ref[0:1, :]
    for k in range(1, wb_hist + 1):
        uc = uc + ubuf[k:k + tt] * cbw_ref[k:k + 1, :]
    yb = proj_ref[:, :, d:2 * d] * uc

    merged = (jax.nn.sigmoid(proj_ref[:, :, 4 * d:5 * d]) * ya
              + jax.nn.sigmoid(proj_ref[:, :, 5 * d:6 * d]) * yb)
    o = jnp.dot(merged.reshape(tt * bb, d).astype(BF16), wout_ref[...],
                preferred_element_type=F32)
    xo_ref[...] = x_ref[...] + g1_ref[...] * o.reshape(tt, bb, d)

    new_a = xbuf[tt:tt + wa_hist]
    new_b = ubuf[tt:tt + wb_hist]
    xbuf[0:wa_hist] = new_a
    ubuf[0:wb_hist] = new_b
    na_ref[...] = new_a
    nb_ref[...] = new_b
    nh_ref[...] = h


def _mixer(is_prompt, proj, x, bufa, h0, bufb, g1, caw, cab, wa, ba, wx, bx, lam, cbw, wout):
    t, b, d = x.shape
    cols = proj.shape[2]
    bb = min(b, 32)
    tt = _time_tile(MIXER_ROWS, t, bb)
    wa_hist, wb_hist = bufa.shape[0], bufb.shape[0]
    row = lambda a: a.reshape(1, d)
    full = lambda shape: pl.BlockSpec(shape, lambda i, j: (0,) * len(shape))
    return pl.pallas_call(
        functools.partial(_mixer_kernel, is_prompt),
        grid=(b // bb, t // tt),
        in_specs=[
            pl.BlockSpec((tt, bb, cols), lambda i, j: (j, i, 0)),
            pl.BlockSpec((tt, bb, d), lambda i, j: (j, i, 0)),
            pl.BlockSpec((wa_hist, bb, d), lambda i, j: (0, i, 0)),
            pl.BlockSpec((bb, d), lambda i, j: (i, 0)),
            pl.BlockSpec((wb_hist, bb, d), lambda i, j: (0, i, 0)),
            pl.BlockSpec((bb, d), lambda i, j: (i, 0)),
            full((wa_hist + 1, d)), full((1, d)),
            full((d, d)), full((1, d)), full((d, d)), full((1, d)), full((1, d)),
            full((wb_hist + 1, d)), full((d, d)),
        ],
        out_specs=[
            pl.BlockSpec((tt, bb, d), lambda i, j: (j, i, 0)),
            pl.BlockSpec((wa_hist, bb, d), lambda i, j: (0, i, 0)),
            pl.BlockSpec((bb, d), lambda i, j: (i, 0)),
            pl.BlockSpec((wb_hist, bb, d), lambda i, j: (0, i, 0)),
        ],
        out_shape=[
            jax.ShapeDtypeStruct((t, b, d), F32),
            jax.ShapeDtypeStruct((wa_hist, b, d), F32),
            jax.ShapeDtypeStruct((b, d), F32),
            jax.ShapeDtypeStruct((wb_hist, b, d), F32),
        ],
        scratch_shapes=[
            pltpu.VMEM((wa_hist + tt, bb, d), F32),
            pltpu.VMEM((wb_hist + tt, bb, d), F32),
            pltpu.VMEM((bb, d), F32),
            pltpu.VMEM((tt, bb, d), F32),
            pltpu.VMEM((tt, bb, d), F32),
        ],
        compiler_params=_params("parallel", "arbitrary"),
        name="mixer",
    )(proj, x, bufa, h0, bufb, g1, caw, row(cab), wa, row(ba), wx, row(bx), row(lam), cbw, wout)


def _topk_rank(s, order, k):
    rank = jnp.full(s.shape, float(k), F32)
    vals = []
    for j in range(k):
        m = jnp.max(s, axis=0, keepdims=True)
        first = jnp.min(jnp.where(s == m, order, jnp.inf), axis=0, keepdims=True)
        sel = order == first
        rank = jnp.where(sel, float(j), rank)
        s = jnp.where(sel, -jnp.inf, s)
        vals.append(m)
    return vals, rank


_CAND_ROWS = TOPK + 7 * 8 + 8


def _cand_flat_index(tl):
    r = lax.broadcasted_iota(jnp.int32, (_CAND_ROWS, tl), 0)
    q = r - TOPK
    mid = TOPK + (q >> 3) * TOPK + (q & 7)
    last = (r - (TOPK + 7 * 8) + 8) * TOPK
    return jnp.where(r < TOPK, r, jnp.where(r < TOPK + 7 * 8, mid, last)).astype(F32)


def _route_chunk(s0, s1):
    tl = s0.shape[1]
    key_order = lax.broadcasted_iota(jnp.int32, (N_KEYS, tl), 0).astype(F32)
    v0, rank0 = _topk_rank(s0, key_order, TOPK)
    v1, rank1 = _topk_rank(s1, key_order, TOPK)
    sv1 = jnp.concatenate(v1, axis=0)
    sv1_lo = sv1[0:8]
    sv0_hi = jnp.concatenate(v0[8:], axis=0)
    cand = jnp.concatenate(
        [v0[0] + sv1] + [v0[a] + sv1_lo for a in range(1, 8)] + [sv0_hi + v1[0]], axis=0)
    cvals, crank = _topk_rank(cand, _cand_flat_index(tl), TOPK)
    chosen = jnp.where(crank < float(TOPK), 1.0, 0.0)
    z = jnp.sum(chosen * jnp.exp(cand - cvals[0]), axis=0, keepdims=True)
    counts = [jnp.sum(chosen[0:TOPK], axis=0, keepdims=True)]
    counts += [jnp.sum(chosen[TOPK + 8 * (a - 1):TOPK + 8 * a], axis=0, keepdims=True)
               for a in range(1, 8)]
    counts += [chosen[TOPK + 7 * 8 + a:TOPK + 7 * 8 + a + 1] for a in range(8)]
    nsel = jnp.zeros(s0.shape, F32)
    for a in range(TOPK):
        nsel = nsel + jnp.where(rank0 == float(a), counts[a], 0.0)
    c = jnp.exp(s0 - v0[0]) / z
    e1 = jnp.exp(s1 - v1[0])
    return nsel, c, rank1, e1


def _router_kernel(x_ref, g_ref, sh_ref, sc_ref, wqt_ref, keys_ref,
                   xt_ref, nf_ref, c_ref, r1_ref, e1_ref, qt_scr):
    tt, b, d = x_ref.shape
    tm = tt * b
    hn = _norm_mod(x_ref[...], g_ref[...], sh_ref[...], sc_ref[...]).reshape(tm, d)
    xt = hn.T.astype(BF16)
    xt_ref[...] = xt
    qt_scr[...] = jnp.dot(wqt_ref[...], xt, preferred_element_type=F32)

    def head(h, carry):
        for lo in range(0, tm, LANES):
            s = []
            for p in range(2):
                row0 = pl.multiple_of((2 * h + p) * N_KEYS, N_KEYS)
                qc = qt_scr[pl.ds(row0, N_KEYS), lo:lo + LANES].astype(BF16)
                s.append(jnp.dot(keys_ref[2 * h + p], qc, preferred_element_type=F32))
            nsel, c, rank1, e1 = _route_chunk(s[0], s[1])
            nf_ref[h, :, lo:lo + LANES] = nsel
            c_ref[h, :, lo:lo + LANES] = c
            r1_ref[h, :, lo:lo + LANES] = pltpu.bitcast(rank1.astype(BF16), jnp.uint32)
            e1_ref[h, :, lo:lo + LANES] = pltpu.bitcast(e1.astype(BF16), jnp.uint32)
        return carry

    lax.fori_loop(0, PEER_HEADS, head, 0)


def _router(x, g, sh, sc, wqt_bf16, keys_bf16):
    t, b, d = x.shape
    n = t * b
    tt = _time_tile(ROUTER_ROW_TILE, t, b)
    tm = tt * b
    dq = wqt_bf16.shape[0]
    assert tm % LANES == 0
    fac = jax.ShapeDtypeStruct((PEER_HEADS, N_KEYS, n), F32)
    fac16 = jax.ShapeDtypeStruct((PEER_HEADS, N_KEYS // 2, n), jnp.uint32)
    fac_spec = pl.BlockSpec((PEER_HEADS, N_KEYS, tm), lambda i: (0, 0, i))
    fac16_spec = pl.BlockSpec((PEER_HEADS, N_KEYS // 2, tm), lambda i: (0, 0, i))
    return pl.pallas_call(
        _router_kernel,
        grid=(t // tt,),
        in_specs=[
            pl.BlockSpec((tt, b, d), lambda i: (i, 0, 0)),
            pl.BlockSpec((1, d), lambda i: (0, 0)),
            pl.BlockSpec((b, d), lambda i: (0, 0)),
            pl.BlockSpec((b, d), lambda i: (0, 0)),
            pl.BlockSpec((dq, d), lambda i: (0, 0)),
            pl.BlockSpec((2 * PEER_HEADS, N_KEYS, N_KEYS), lambda i: (0, 0, 0)),
        ],
        out_specs=[pl.BlockSpec((d, tm), lambda i: (0, i)), fac_spec, fac_spec,
                   fac16_spec, fac16_spec],
        out_shape=[jax.ShapeDtypeStruct((d, n), BF16), fac, fac, fac16, fac16],
        scratch_shapes=[pltpu.VMEM((dq, tm), F32)],
        compiler_params=_params("parallel"),
        name="peer_router",
    )(x, g, sh, sc, wqt_bf16, keys_bf16)


def _bcast_rows_bf16(row):
    return jnp.broadcast_to(row, (BF16_ROWS, row.shape[1])).astype(BF16)


def _peer_kernel(xt_ref, u_ref, vt_ref, nf_ref, c_ref, r1_ref, e1_ref, x_ref, g2_ref,
                 o_ref, acc_ref, s_ref, w_ref):
    tt, b, d = x_ref.shape
    tm = tt * b
    tn = u_ref.shape[0]
    groups = tn // N_KEYS
    assert 2 * groups == SUBLANES
    k = pl.program_id(1)
    n_tiles = pl.num_programs(1) - 2

    @pl.when(k == 0)
    def _():
        acc_ref[...] = jnp.zeros_like(acc_ref)
        s_ref[...] = jnp.zeros_like(s_ref)
        w_ref[...] = jnp.zeros_like(w_ref)

    tile = jnp.clip(k - 1, 0, n_tiles - 1)

    def gate_chunk(prev, lo):
        lanes = slice(lo, lo + LANES)
        reps = N_KEYS // BF16_ROWS
        gates = [None] * groups
        base = pl.multiple_of((tile * groups // SUBLANES) * SUBLANES, SUBLANES)
        for h in range(PEER_HEADS):
            r1 = pltpu.bitcast(r1_ref[h, :, lanes], BF16)
            e1 = pltpu.bitcast(e1_ref[h, :, lanes], BF16)
            n_rows = nf_ref[h, pl.ds(base, SUBLANES), lanes]
            c_rows = c_ref[h, pl.ds(base, SUBLANES), lanes]
            for ii in range(groups):
                r = prev * groups + ii
                n_b = jnp.tile(_bcast_rows_bf16(n_rows[r:r + 1]), (reps, 1))
                c_b = jnp.tile(_bcast_rows_bf16(c_rows[r:r + 1]), (reps, 1))
                term = c_b * jnp.where(r1 < n_b, e1, 0.0)
                gates[ii] = term if gates[ii] is None else gates[ii] + term
        for ii in range(groups):
            rows = slice(ii * N_KEYS, (ii + 1) * N_KEYS)
            s = s_ref[prev, rows, lanes]
            act = (0.5 * s) * (1.0 + lax.erf(s * (0.5 ** 0.5)))
            w_ref[prev, rows, lanes] = gates[ii] * act.astype(BF16)

    def step(cur):
        prev = 1 - cur
        for lo in range(0, tm, MXU_COLS):
            cols = slice(lo, lo + MXU_COLS)
            acc_ref[:, cols] += jnp.dot(vt_ref[...], w_ref[cur, :, cols],
                                        preferred_element_type=F32)
            for sub in range(lo, lo + MXU_COLS, LANES):
                gate_chunk(prev, sub)
            s_ref[cur, :, cols] = jnp.dot(u_ref[...], xt_ref[:, cols],
                                          preferred_element_type=F32)

    @pl.when(k % 2 == 0)
    def _():
        step(0)

    @pl.when(k % 2 == 1)
    def _():
        step(1)

    @pl.when(k == n_tiles + 1)
    def _():
        o = acc_ref[...].T.reshape(tt, b, d)
        o_ref[...] = x_ref[...] + g2_ref[...] * o


def _peer(xt, u_bf16, vt_bf16, nf, c, r1, e1, x, g2):
    t, b, d = x.shape
    n = t * b
    ne = u_bf16.shape[0]
    tt = _time_tile(ROW_TILE, t, b)
    tm = tt * b
    tn = EXPERT_TILE
    n_tiles = ne // tn
    fac_spec = pl.BlockSpec((PEER_HEADS, N_KEYS, tm), lambda i, j: (0, 0, i))
    fac16_spec = pl.BlockSpec((PEER_HEADS, N_KEYS // 2, tm), lambda i, j: (0, 0, i))
    return pl.pallas_call(
        _peer_kernel,
        grid=(n // tm, n_tiles + 2),
        in_specs=[
            pl.BlockSpec((d, tm), lambda i, j: (0, i)),
            pl.BlockSpec((tn, d), lambda i, j: (jnp.minimum(j, n_tiles - 1), 0)),
            pl.BlockSpec((d, tn), lambda i, j: (0, jnp.clip(j - 2, 0, n_tiles - 1))),
            fac_spec, fac_spec, fac16_spec, fac16_spec,
            pl.BlockSpec((tt, b, d), lambda i, j: (i, 0, 0)),
            pl.BlockSpec((b, d), lambda i, j: (0, 0)),
        ],
        out_specs=pl.BlockSpec((tt, b, d), lambda i, j: (i, 0, 0)),
        out_shape=jax.ShapeDtypeStruct((t, b, d), F32),
        scratch_shapes=[
            pltpu.VMEM((d, tm), F32),
            pltpu.VMEM((2, tn, tm), F32),
            pltpu.VMEM((2, tn, tm), BF16),
        ],
        compiler_params=_params("parallel", "arbitrary"),
        name="peer_experts",
    )(xt, u_bf16, vt_bf16, nf, c, r1, e1, x, g2)


def _final_norm_kernel(x_ref, g_ref, o_ref):
    x = x_ref[...]
    var = jnp.mean(x * x, axis=-1, keepdims=True)
    o_ref[...] = (x * lax.rsqrt(var + EPS)) * g_ref[...]


def _final_norm(x, g):
    t, b, d = x.shape
    tt = _time_tile(ROW_TILE, t, b)
    return pl.pallas_call(
        _final_norm_kernel,
        grid=(t // tt,),
        in_specs=[pl.BlockSpec((tt, b, d), lambda i: (i, 0, 0)),
                  pl.BlockSpec((1, d), lambda i: (0, 0))],
        out_specs=pl.BlockSpec((tt, b, d), lambda i: (i, 0, 0)),
        out_shape=jax.ShapeDtypeStruct((t, b, d), F32),
        compiler_params=_params("parallel"),
        name="final_norm",
    )(x, g)


def _block_diag(w):
    nb, bs, _ = w.shape
    eye = jnp.eye(nb, dtype=w.dtype)
    return (eye[:, None, :, None] * w[:, :, None, :]).reshape(nb * bs, nb * bs)


def _layer(x, mod, states, is_prompt, lw):
    sh1, sc1, g1, sh2, sc2, g2 = mod
    bufa, h0, bufb = states
    proj = _inproj(x, lw["norm1"], sh1, sc1, lw["w_in"])
    x, na, nh, nb = _mixer(is_prompt, proj, x, bufa, h0, bufb, g1,
                           lw["conv_a_w"], lw["conv_a_b"], lw["rg_w_a"], lw["rg_b_a"],
                           lw["rg_w_x"], lw["rg_b_x"], lw["rg_lambda"], lw["conv_b_w"], lw["w_out"])
    xt, nf, c, r1, e1 = _router(x, lw["norm2"], sh2, sc2, lw["w_qt"], lw["keys"])
    x = _peer(xt, lw["u"], lw["vt"], nf, c, r1, e1, x, g2)
    return x, na, nh, nb


def kernel(x_prompt, x_sample, c_prompt, c_sample, state_conv_a, state_h, state_conv_b, w_ada, b_ada, norm1, norm2, w_in, conv_a_w, conv_a_b, rg_w_a, rg_b_a, rg_w_x, rg_b_x, rg_lambda, conv_b_w, w_out, peer_w_q, peer_sub_keys, peer_u, peer_v, final_norm):
    depth = w_ada.shape[0]
    bp, _, d = x_prompt.shape
    bs = x_sample.shape[0]
    wa_hist = conv_a_w.shape[1] - 1
    wb_hist = conv_b_w.shape[1] - 1

    mods = _ada(jnp.concatenate([c_prompt, c_sample], axis=0), w_ada, b_ada)
    mods = mods.reshape(depth, bp + bs, N_MOD, d)

    xp = jnp.transpose(x_prompt, (1, 0, 2))
    xs = jnp.transpose(x_sample, (1, 0, 2))
    zeros_p = (jnp.zeros((wa_hist, bp, d), F32), jnp.zeros((bp, d), F32),
               jnp.zeros((wb_hist, bp, d), F32))
    outs_p, outs_s = [], []
    for l in range(depth):
        lw = {
            "norm1": norm1[l].reshape(1, d), "norm2": norm2[l].reshape(1, d),
            "w_in": w_in[l].astype(BF16),
            "conv_a_w": conv_a_w[l], "conv_a_b": conv_a_b[l],
            "rg_w_a": _block_diag(rg_w_a[l]).astype(BF16), "rg_b_a": rg_b_a[l],
            "rg_w_x": _block_diag(rg_w_x[l]).astype(BF16), "rg_b_x": rg_b_x[l],
            "rg_lambda": rg_lambda[l], "conv_b_w": conv_b_w[l],
            "w_out": w_out[l].astype(BF16),
            "w_qt": peer_w_q[l].T.astype(BF16),
            "keys": peer_sub_keys[l].reshape(2 * PEER_HEADS, N_KEYS, -1).astype(BF16),
            "u": peer_u[l].astype(BF16),
            "vt": peer_v[l].T.astype(BF16),
        }
        mod_p = [mods[l, :bp, k] for k in range(N_MOD)]
        mod_s = [mods[l, bp:, k] for k in range(N_MOD)]
        xp, na, nh, nb = _layer(xp, mod_p, zeros_p, True, lw)
        outs_p.append((na, nh, nb))
        states_s = (jnp.transpose(state_conv_a[l], (1, 0, 2)), state_h[l],
                    jnp.transpose(state_conv_b[l], (1, 0, 2)))
        xs, na, nh, nb = _layer(xs, mod_s, states_s, False, lw)
        outs_s.append((na, nh, nb))

    fn = final_norm.reshape(1, d)
    y_prompt = jnp.transpose(_final_norm(xp, fn), (1, 0, 2))
    y_sample = jnp.transpose(_final_norm(xs, fn), (1, 0, 2))

    def stack(outs):
        ca = jnp.stack([jnp.transpose(o[0], (1, 0, 2)) for o in outs], 0)
        hh = jnp.stack([o[1] for o in outs], 0)
        cb = jnp.stack([jnp.transpose(o[2], (1, 0, 2)) for o in outs], 0)
        return ca, hh, cb

    ca_p, hh_p, cb_p = stack(outs_p)
    ca_s, hh_s, cb_s = stack(outs_s)
    return (y_prompt, y_sample, ca_p, hh_p, cb_p, ca_s, hh_s, cb_s)
```

```python
import functools

import jax
import jax.numpy as jnp
from jax import lax
from jax.experimental import pallas as pl
from jax.experimental.pallas import tpu as pltpu

F32 = jnp.float32
BF16 = jnp.bfloat16

EPS = 1e-6
C_LRU = 8.0
N_RNN_BLOCKS = 16
N_KEYS = 128
PEER_HEADS = 8
TOPK = 16
N_MOD = 6

VMEM_LIMIT_BYTES = 56 * 1024 * 1024
ROW_TILE = 512
ROUTER_ROW_TILE = 256
PEER_ROW_TILE = 1024
EXPERT_TILE = 512
MIXER_ROWS = 256
LANES = 128
SUBLANES = 8
BF16_ROWS = 16
MXU_COLS = 256


def _params(*sem):
    return pltpu.CompilerParams(dimension_semantics=sem, vmem_limit_bytes=VMEM_LIMIT_BYTES)


def _time_tile(rows, t, b):
    tt = max(1, min(t, rows // b))
    assert t % tt == 0, (t, tt)
    return tt


def _ada_kernel(c_ref, w_ref, b_ref, o_ref):
    c = c_ref[...].astype(BF16)
    w = w_ref[...].astype(BF16)
    o_ref[...] = jnp.dot(c, w, preferred_element_type=F32) + b_ref[...]


def _ada(c_all, w_ada, b_ada):
    depth, d, cols = w_ada.shape
    bc = c_all.shape[0]
    tn = 1536
    assert cols % tn == 0
    return pl.pallas_call(
        _ada_kernel,
        grid=(depth, cols // tn),
        in_specs=[
            pl.BlockSpec((bc, d), lambda l, j: (0, 0)),
            pl.BlockSpec((None, d, tn), lambda l, j: (l, 0, j)),
            pl.BlockSpec((None, 1, tn), lambda l, j: (l, 0, j)),
        ],
        out_specs=pl.BlockSpec((None, bc, tn), lambda l, j: (l, 0, j)),
        out_shape=jax.ShapeDtypeStruct((depth, bc, cols), F32),
        compiler_params=_params("parallel", "parallel"),
        name="ada_mod",
    )(c_all, w_ada, b_ada.reshape(depth, 1, cols))


def _norm_mod(x, g, sh, sc):
    var = jnp.mean(x * x, axis=-1, keepdims=True)
    xn = (x * lax.rsqrt(var + EPS)) * g
    return xn * (1.0 + sc) + sh


def _inproj_kernel(x_ref, g_ref, sh_ref, sc_ref, w_ref, o_ref, hn_ref):
    tt, b, d = x_ref.shape

    @pl.when(pl.program_id(1) == 0)
    def _():
        hn = _norm_mod(x_ref[...], g_ref[...], sh_ref[...], sc_ref[...])
        hn_ref[...] = hn.reshape(tt * b, d).astype(BF16)

    y = jnp.dot(hn_ref[...], w_ref[...], preferred_element_type=F32)
    o_ref[...] = y.reshape(tt, b, -1)


def _inproj(x, g, sh, sc, w_bf16):
    t, b, d = x.shape
    cols = w_bf16.shape[1]
    tt = _time_tile(ROW_TILE, t, b)
    tn = 1536
    return pl.pallas_call(
        _inproj_kernel,
        grid=(t // tt, cols // tn),
        in_specs=[
            pl.BlockSpec((tt, b, d), lambda i, j: (i, 0, 0)),
            pl.BlockSpec((1, d), lambda i, j: (0, 0)),
            pl.BlockSpec((b, d), lambda i, j: (0, 0)),
            pl.BlockSpec((b, d), lambda i, j: (0, 0)),
            pl.BlockSpec((d, tn), lambda i, j: (0, j)),
        ],
        out_specs=pl.BlockSpec((tt, b, tn), lambda i, j: (i, 0, j)),
        out_shape=jax.ShapeDtypeStruct((t, b, cols), F32),
        scratch_shapes=[pltpu.VMEM((tt * b, d), BF16)],
        compiler_params=_params("parallel", "arbitrary"),
        name="in_proj",
    )(x, g, sh, sc, w_bf16)


def _mixer_kernel(is_prompt, proj_ref, x_ref, bufa_ref, h0_ref, bufb_ref, g1_ref,
                  caw_ref, cab_ref, wa_ref, ba_ref, wx_ref, bx_ref, lam_ref, cbw_ref, wout_ref,
                  xo_ref, na_ref, nh_ref, nb_ref,
                  xbuf, ubuf, h_scr, a_scr, b_scr):
    tt, bb, d = x_ref.shape
    wa_hist = bufa_ref.shape[0]
    wb_hist = bufb_ref.shape[0]
    ti = pl.program_id(1)

    @pl.when(ti == 0)
    def _():
        xbuf[0:wa_hist] = bufa_ref[...]
        ubuf[0:wb_hist] = bufb_ref[...]
        h_scr[...] = h0_ref[...]

    xa = proj_ref[:, :, 0:d]
    xbuf[wa_hist:wa_hist + tt] = xa
    xc = xbuf[0:tt] * caw_ref[0:1, :]
    for k in range(1, wa_hist + 1):
        xc = xc + xbuf[k:k + tt] * caw_ref[k:k + 1, :]
    xc = xc + cab_ref[...]
    xc2 = xc.reshape(tt * bb, d)
    xcb = xc2.astype(BF16)
    r = jax.nn.sigmoid(jnp.dot(xcb, wa_ref[...], preferred_element_type=F32) + ba_ref[...])
    ig = jax.nn.sigmoid(jnp.dot(xcb, wx_ref[...], preferred_element_type=F32) + bx_ref[...])
    nl = -lam_ref[...]
    softplus = jnp.maximum(nl, 0.0) + jnp.log1p(jnp.exp(-jnp.abs(nl)))
    log_a = (-C_LRU * r) * softplus
    a = jnp.exp(log_a)
    mult = jnp.sqrt(1.0 - a * a)
    if is_prompt:
        row = lax.broadcasted_iota(jnp.int32, (tt * bb, 1), 0)
        mult = jnp.where((row < bb) & (ti == 0), 1.0, mult)
    a_scr[...] = a.reshape(tt, bb, d)
    b_scr[...] = (mult * (ig * xc2)).reshape(tt, bb, d)

    def step(t, h):
        h = a_scr[t] * h + b_scr[t]
        b_scr[t] = h
        return h

    h = lax.fori_loop(0, tt, step, h_scr[...])
    h_scr[...] = h
    ya = b_scr[...]

    u = proj_ref[:, :, 2 * d:3 * d] * proj_ref[:, :, 3 * d:4 * d]
    ubuf[wb_hist:wb_hist + tt] = u
    uc = ubuf[0:tt] * cbw_ref[0:1, :]
    for k in range(1, wb_hist + 1):
        uc = uc + ubuf[k:k + tt] * cbw_ref[k:k + 1, :]
    yb = proj_ref[:, :, d:2 * d] * uc

    merged = (jax.nn.sigmoid(proj_ref[:, :, 4 * d:5 * d]) * ya
              + jax.nn.sigmoid(proj_ref[:, :, 5 * d:6 * d]) * yb)
    o = jnp.dot(merged.reshape(tt * bb, d).astype(BF16), wout_ref[...],
                preferred_element_type=F32)
    xo_ref[...] = x_ref[...] + g1_ref[...] * o.reshape(tt, bb, d)

    new_a = xbuf[tt:tt + wa_hist]
    new_b = ubuf[tt:tt + wb_hist]
    xbuf[0:wa_hist] = new_a
    ubuf[0:wb_hist] = new_b
    na_ref[...] = new_a
    nb_ref[...] = new_b
    nh_ref[...] = h


def _mixer(is_prompt, proj, x, bufa, h0, bufb, g1, caw, cab, wa, ba, wx, bx, lam, cbw, wout):
    t, b, d = x.shape
    cols = proj.shape[2]
    bb = min(b, 32)
    tt = _time_tile(MIXER_ROWS, t, bb)
    wa_hist, wb_hist = bufa.shape[0], bufb.shape[0]
    row = lambda a: a.reshape(1, d)
    full = lambda shape: pl.BlockSpec(shape, lambda i, j: (0,) * len(shape))
    return pl.pallas_call(
        functools.partial(_mixer_kernel, is_prompt),
        grid=(b // bb, t // tt),
        in_specs=[
            pl.BlockSpec((tt, bb, cols), lambda i, j: (j, i, 0)),
            pl.BlockSpec((tt, bb, d), lambda i, j: (j, i, 0)),
            pl.BlockSpec((wa_hist, bb, d), lambda i, j: (0, i, 0)),
            pl.BlockSpec((bb, d), lambda i, j: (i, 0)),
            pl.BlockSpec((wb_hist, bb, d), lambda i, j: (0, i, 0)),
            pl.BlockSpec((bb, d), lambda i, j: (i, 0)),
            full((wa_hist + 1, d)), full((1, d)),
            full((d, d)), full((1, d)), full((d, d)), full((1, d)), full((1, d)),
            full((wb_hist + 1, d)), full((d, d)),
        ],
        out_specs=[
            pl.BlockSpec((tt, bb, d), lambda i, j: (j, i, 0)),
            pl.BlockSpec((wa_hist, bb, d), lambda i, j: (0, i, 0)),
            pl.BlockSpec((bb, d), lambda i, j: (i, 0)),
            pl.BlockSpec((wb_hist, bb, d), lambda i, j: (0, i, 0)),
        ],
        out_shape=[
            jax.ShapeDtypeStruct((t, b, d), F32),
            jax.ShapeDtypeStruct((wa_hist, b, d), F32),
            jax.ShapeDtypeStruct((b, d), F32),
            jax.ShapeDtypeStruct((wb_hist, b, d), F32),
        ],
        scratch_shapes=[
            pltpu.VMEM((wa_hist + tt, bb, d), F32),
            pltpu.VMEM((wb_hist + tt, bb, d), F32),
            pltpu.VMEM((bb, d), F32),
            pltpu.VMEM((tt, bb, d), F32),
            pltpu.VMEM((tt, bb, d), F32),
        ],
        compiler_params=_params("parallel", "arbitrary"),
        name="mixer",
    )(proj, x, bufa, h0, bufb, g1, caw, row(cab), wa, row(ba), wx, row(bx), row(lam), cbw, wout)


def _topk_rank(s, order, k):
    rank = jnp.full(s.shape, float(k), F32)
    vals = []
    for j in range(k):
        m = jnp.max(s, axis=0, keepdims=True)
        first = jnp.min(jnp.where(s == m, order, jnp.inf), axis=0, keepdims=True)
        sel = order == first
        rank = jnp.where(sel, float(j), rank)
        s = jnp.where(sel, -jnp.inf, s)
        vals.append(m)
    return vals, rank


def _topk_values(s, k):
    vals = []
    for _ in range(k):
        m = jnp.max(s, axis=0, keepdims=True)
        s = jnp.where(s == m, -jnp.inf, s)
        vals.append(m)
    return vals


def _rank_from_values(s, vals):
    rank = jnp.full(s.shape, float(len(vals)), F32)
    for j, v in enumerate(vals):
        rank = jnp.where(s == v, float(j), rank)
    return rank


_CAND_ROWS = TOPK + 7 * 8 + 8


def _cand_flat_index(tl):
    r = lax.broadcasted_iota(jnp.int32, (_CAND_ROWS, tl), 0)
    q = r - TOPK
    mid = TOPK + (q >> 3) * TOPK + (q & 7)
    last = (r - (TOPK + 7 * 8) + 8) * TOPK
    return jnp.where(r < TOPK, r, jnp.where(r < TOPK + 7 * 8, mid, last)).astype(F32)


def _route_chunk(s0, s1, v0, rank0, v1, rank1, exact):
    tl = s0.shape[1]
    sv1 = jnp.concatenate(v1, axis=0)
    sv1_lo = sv1[0:8]
    sv0_hi = jnp.concatenate(v0[8:], axis=0)
    cand = jnp.concatenate(
        [v0[0] + sv1] + [v0[a] + sv1_lo for a in range(1, 8)] + [sv0_hi + v1[0]], axis=0)
    if exact:
        _, crank = _topk_rank(cand, _cand_flat_index(tl), TOPK)
        chosen = jnp.where(crank < float(TOPK), 1.0, 0.0)
    else:
        chosen = jnp.where(cand >= _topk_values(cand, TOPK)[-1], 1.0, 0.0)
    z = jnp.sum(chosen * jnp.exp(cand - (v0[0] + v1[0])), axis=0, keepdims=True)
    counts = [jnp.sum(chosen[0:TOPK], axis=0, keepdims=True)]
    counts += [jnp.sum(chosen[TOPK + 8 * (a - 1):TOPK + 8 * a], axis=0, keepdims=True)
               for a in range(1, 8)]
    counts += [chosen[TOPK + 7 * 8 + a:TOPK + 7 * 8 + a + 1] for a in range(8)]
    nsel = jnp.zeros(s0.shape, F32)
    for a in range(TOPK):
        nsel = nsel + jnp.where(rank0 == float(a), counts[a], 0.0)
    c = jnp.exp(s0 - v0[0]) / z
    e1 = jnp.exp(s1 - v1[0])
    return nsel, c, rank1, e1, jnp.sum(chosen, axis=0, keepdims=True)


def _router_kernel(x_ref, g_ref, sh_ref, sc_ref, wqt_ref, keys_ref,
                   xt_ref, nf_ref, c_ref, r1_ref, e1_ref, qt_scr):
    tt, b, d = x_ref.shape
    tm = tt * b
    hn = _norm_mod(x_ref[...], g_ref[...], sh_ref[...], sc_ref[...]).reshape(tm, d)
    xt = hn.T.astype(BF16)
    xt_ref[...] = xt
    qt_scr[...] = jnp.dot(wqt_ref[...], xt, preferred_element_type=F32)

    def head(h, carry):
        def scores(lanes):
            s = []
            for p in range(2):
                row0 = pl.multiple_of((2 * h + p) * N_KEYS, N_KEYS)
                qc = qt_scr[pl.ds(row0, N_KEYS), lanes].astype(BF16)
                s.append(jnp.dot(keys_ref[2 * h + p], qc, preferred_element_type=F32))
            return s

        def emit(lanes, s, v0, rank0, v1, rank1, exact):
            nsel, c, rank1, e1, n_chosen = _route_chunk(s[0], s[1], v0, rank0, v1, rank1, exact)
            nf_ref[h, :, lanes] = nsel
            c_ref[h, :, lanes] = c
            r1_ref[h, :, lanes] = pltpu.bitcast(rank1.astype(BF16), jnp.uint32)
            e1_ref[h, :, lanes] = pltpu.bitcast(e1.astype(BF16), jnp.uint32)
            return n_chosen

        tied = []
        for lo in range(0, tm, LANES):
            lanes = slice(lo, lo + LANES)
            s = scores(lanes)
            v0 = _topk_values(s[0], TOPK)
            v1 = _topk_values(s[1], TOPK)
            n_chosen = emit(lanes, s, v0, _rank_from_values(s[0], v0),
                            v1, _rank_from_values(s[1], v1), False)
            reach = [jnp.sum(jnp.where(s[p] >= v[-1], 1.0, 0.0), axis=0, keepdims=True)
                     for p, v in enumerate((v0, v1))]
            tied.append(jnp.max(jnp.maximum(jnp.maximum(reach[0], reach[1]), n_chosen))
                        > float(TOPK))

        for lo, redo in zip(range(0, tm, LANES), tied):
            @pl.when(redo)
            def _(lo=lo):
                lanes = slice(lo, lo + LANES)
                s = scores(lanes)
                key_order = lax.broadcasted_iota(jnp.int32, (N_KEYS, LANES), 0).astype(F32)
                emit(lanes, s, *_topk_rank(s[0], key_order, TOPK),
                     *_topk_rank(s[1], key_order, TOPK), True)
        return carry

    lax.fori_loop(0, PEER_HEADS, head, 0)


def _router(x, g, sh, sc, wqt_bf16, keys_bf16):
    t, b, d = x.shape
    n = t * b
    tt = _time_tile(ROUTER_ROW_TILE, t, b)
    tm = tt * b
    dq = wqt_bf16.shape[0]
    assert tm % LANES == 0
    fac = jax.ShapeDtypeStruct((PEER_HEADS, N_KEYS, n), F32)
    fac16 = jax.ShapeDtypeStruct((PEER_HEADS, N_KEYS // 2, n), jnp.uint32)
    fac_spec = pl.BlockSpec((PEER_HEADS, N_KEYS, tm), lambda i: (0, 0, i))
    fac16_spec = pl.BlockSpec((PEER_HEADS, N_KEYS // 2, tm), lambda i: (0, 0, i))
    return pl.pallas_call(
        _router_kernel,
        grid=(t // tt,),
        in_specs=[
            pl.BlockSpec((tt, b, d), lambda i: (i, 0, 0)),
            pl.BlockSpec((1, d), lambda i: (0, 0)),
            pl.BlockSpec((b, d), lambda i: (0, 0)),
            pl.BlockSpec((b, d), lambda i: (0, 0)),
            pl.BlockSpec((dq, d), lambda i: (0, 0)),
            pl.BlockSpec((2 * PEER_HEADS, N_KEYS, N_KEYS), lambda i: (0, 0, 0)),
        ],
        out_specs=[pl.BlockSpec((d, tm), lambda i: (0, i)), fac_spec, fac_spec,
                   fac16_spec, fac16_spec],
        out_shape=[jax.ShapeDtypeStruct((d, n), BF16), fac, fac, fac16, fac16],
        scratch_shapes=[pltpu.VMEM((dq, tm), F32)],
        compiler_params=_params("parallel"),
        name="peer_router",
    )(x, g, sh, sc, wqt_bf16, keys_bf16)


def _bcast_rows_bf16(row):
    return jnp.broadcast_to(row, (BF16_ROWS, row.shape[1])).astype(BF16)


def _peer_kernel(xt_ref, u_ref, vt_ref, nf_ref, c_ref, r1_ref, e1_ref, x_ref, g2_ref,
                 o_ref, acc_ref, act_ref, w_ref):
    tt, b, d = x_ref.shape
    tm = tt * b
    tn = u_ref.shape[0]
    groups = tn // N_KEYS
    assert 2 * groups == SUBLANES
    k = pl.program_id(1)
    n_tiles = pl.num_programs(1) - 1

    @pl.when(k == 0)
    def _():
        acc_ref[...] = jnp.zeros_like(acc_ref)
        act_ref[...] = jnp.zeros_like(act_ref)

    def gate_chunk(prev, lo):
        lanes = slice(lo, lo + LANES)
        reps = N_KEYS // BF16_ROWS
        gates = [None] * groups
        for h in range(PEER_HEADS):
            r1 = pltpu.bitcast(r1_ref[h, :, lanes], BF16)
            e1 = pltpu.bitcast(e1_ref[h, :, lanes], BF16)
            n_rows = nf_ref[h, :, lanes]
            c_rows = c_ref[h, :, lanes]
            for ii in range(groups):
                r = prev * groups + ii
                n_b = jnp.tile(_bcast_rows_bf16(n_rows[r:r + 1]), (reps, 1))
                c_b = jnp.tile(_bcast_rows_bf16(c_rows[r:r + 1]), (reps, 1))
                term = c_b * jnp.where(r1 < n_b, e1, 0.0)
                gates[ii] = term if gates[ii] is None else gates[ii] + term
        for ii in range(groups):
            rows = slice(ii * N_KEYS, (ii + 1) * N_KEYS)
            w_ref[rows, lanes] = gates[ii] * act_ref[prev, rows, lanes]

    def step(cur):
        prev = 1 - cur
        for lo in range(0, tm, MXU_COLS):
            cols = slice(lo, lo + MXU_COLS)
            s = jnp.dot(u_ref[...], xt_ref[:, cols], preferred_element_type=F32)
            act = (0.5 * s) * (1.0 + lax.erf(s * (0.5 ** 0.5)))
            act_ref[cur, :, cols] = act.astype(BF16)
            for sub in range(lo, lo + MXU_COLS, LANES):
                gate_chunk(prev, sub)
            acc_ref[:, cols] += jnp.dot(vt_ref[...], w_ref[:, cols],
                                        preferred_element_type=F32)

    @pl.when(k % 2 == 0)
    def _():
        step(0)

    @pl.when(k % 2 == 1)
    def _():
        step(1)

    @pl.when(k == n_tiles)
    def _():
        o = acc_ref[...].T.reshape(tt, b, d)
        o_ref[...] = x_ref[...] + g2_ref[...] * o


def _peer(xt, u_bf16, vt_bf16, nf, c, r1, e1, x, g2):
    t, b, d = x.shape
    n = t * b
    ne = u_bf16.shape[0]
    tt = _time_tile(PEER_ROW_TILE, t, b)
    tm = tt * b
    tn = EXPERT_TILE
    n_tiles = ne // tn
    groups = tn // N_KEYS
    fac_spec = pl.BlockSpec(
        (PEER_HEADS, SUBLANES, tm),
        lambda i, j: (0, jnp.clip(j - 1, 0, n_tiles - 1) * groups // SUBLANES, i))
    fac16_spec = pl.BlockSpec((PEER_HEADS, N_KEYS // 2, tm), lambda i, j: (0, 0, i))
    return pl.pallas_call(
        _peer_kernel,
        grid=(n // tm, n_tiles + 1),
        in_specs=[
            pl.BlockSpec((d, tm), lambda i, j: (0, i)),
            pl.BlockSpec((tn, d), lambda i, j: (jnp.minimum(j, n_tiles - 1), 0)),
            pl.BlockSpec((d, tn), lambda i, j: (0, jnp.clip(j - 1, 0, n_tiles - 1))),
            fac_spec, fac_spec, fac16_spec, fac16_spec,
            pl.BlockSpec((tt, b, d), lambda i, j: (i, 0, 0)),
            pl.BlockSpec((b, d), lambda i, j: (0, 0)),
        ],
        out_specs=pl.BlockSpec((tt, b, d), lambda i, j: (i, 0, 0)),
        out_shape=jax.ShapeDtypeStruct((t, b, d), F32),
        scratch_shapes=[
            pltpu.VMEM((d, tm), F32),
            pltpu.VMEM((2, tn, tm), BF16),
            pltpu.VMEM((tn, tm), BF16),
        ],
        compiler_params=_params("parallel", "arbitrary"),
        name="peer_experts",
    )(xt, u_bf16, vt_bf16, nf, c, r1, e1, x, g2)


def _final_norm_kernel(x_ref, g_ref, o_ref):
    x = x_ref[...]
    var = jnp.mean(x * x, axis=-1, keepdims=True)
    o_ref[...] = (x * lax.rsqrt(var + EPS)) * g_ref[...]


def _final_norm(x, g):
    t, b, d = x.shape
    tt = _time_tile(ROW_TILE, t, b)
    return pl.pallas_call(
        _final_norm_kernel,
        grid=(t // tt,),
        in_specs=[pl.BlockSpec((tt, b, d), lambda i: (i, 0, 0)),
                  pl.BlockSpec((1, d), lambda i: (0, 0))],
        out_specs=pl.BlockSpec((tt, b, d), lambda i: (i, 0, 0)),
        out_shape=jax.ShapeDtypeStruct((t, b, d), F32),
        compiler_params=_params("parallel"),
        name="final_norm",
    )(x, g)


def _block_diag(w):
    nb, bs, _ = w.shape
    eye = jnp.eye(nb, dtype=w.dtype)
    return (eye[:, None, :, None] * w[:, :, None, :]).reshape(nb * bs, nb * bs)


def _layer(x, mod, states, is_prompt, lw):
    sh1, sc1, g1, sh2, sc2, g2 = mod
    bufa, h0, bufb = states
    proj = _inproj(x, lw["norm1"], sh1, sc1, lw["w_in"])
    x, na, nh, nb = _mixer(is_prompt, proj, x, bufa, h0, bufb, g1,
                           lw["conv_a_w"], lw["conv_a_b"], lw["rg_w_a"], lw["rg_b_a"],
                           lw["rg_w_x"], lw["rg_b_x"], lw["rg_lambda"], lw["conv_b_w"], lw["w_out"])
    xt, nf, c, r1, e1 = _router(x, lw["norm2"], sh2, sc2, lw["w_qt"], lw["keys"])
    x = _peer(xt, lw["u"], lw["vt"], nf, c, r1, e1, x, g2)
    return x, na, nh, nb


def kernel(x_prompt, x_sample, c_prompt, c_sample, state_conv_a, state_h, state_conv_b, w_ada, b_ada, norm1, norm2, w_in, conv_a_w, conv_a_b, rg_w_a, rg_b_a, rg_w_x, rg_b_x, rg_lambda, conv_b_w, w_out, peer_w_q, peer_sub_keys, peer_u, peer_v, final_norm):
    depth = w_ada.shape[0]
    bp, _, d = x_prompt.shape
    bs = x_sample.shape[0]
    wa_hist = conv_a_w.shape[1] - 1
    wb_hist = conv_b_w.shape[1] - 1

    mods = _ada(jnp.concatenate([c_prompt, c_sample], axis=0), w_ada, b_ada)
    mods = mods.reshape(depth, bp + bs, N_MOD, d)

    xp = jnp.transpose(x_prompt, (1, 0, 2))
    xs = jnp.transpose(x_sample, (1, 0, 2))
    zeros_p = (jnp.zeros((wa_hist, bp, d), F32), jnp.zeros((bp, d), F32),
               jnp.zeros((wb_hist, bp, d), F32))
    outs_p, outs_s = [], []
    for l in range(depth):
        lw = {
            "norm1": norm1[l].reshape(1, d), "norm2": norm2[l].reshape(1, d),
            "w_in": w_in[l].astype(BF16),
            "conv_a_w": conv_a_w[l], "conv_a_b": conv_a_b[l],
            "rg_w_a": _block_diag(rg_w_a[l]).astype(BF16), "rg_b_a": rg_b_a[l],
            "rg_w_x": _block_diag(rg_w_x[l]).astype(BF16), "rg_b_x": rg_b_x[l],
            "rg_lambda": rg_lambda[l], "conv_b_w": conv_b_w[l],
            "w_out": w_out[l].astype(BF16),
            "w_qt": peer_w_q[l].T.astype(BF16),
            "keys": peer_sub_keys[l].reshape(2 * PEER_HEADS, N_KEYS, -1).astype(BF16),
            "u": peer_u[l].astype(BF16),
            "vt": peer_v[l].T.astype(BF16),
        }
        mod_p = [mods[l, :bp, k] for k in range(N_MOD)]
        mod_s = [mods[l, bp:, k] for k in range(N_MOD)]
        xp, na, nh, nb = _layer(xp, mod_p, zeros_p, True, lw)
        outs_p.append((na, nh, nb))
        states_s = (jnp.transpose(state_conv_a[l], (1, 0, 2)), state_h[l],
                    jnp.transpose(state_conv_b[l], (1, 0, 2)))
        xs, na, nh, nb = _layer(xs, mod_s, states_s, False, lw)
        outs_s.append((na, nh, nb))

    fn = final_norm.reshape(1, d)
    y_prompt = jnp.transpose(_final_norm(xp, fn), (1, 0, 2))
    y_sample = jnp.transpose(_final_norm(xs, fn), (1, 0, 2))

    def stack(outs):
        ca = jnp.stack([jnp.transpose(o[0], (1, 0, 2)) for o in outs], 0)
        hh = jnp.stack([o[1] for o in outs], 0)
        cb = jnp.stack([jnp.transpose(o[2], (1, 0, 2)) for o in outs], 0)
        return ca, hh, cb

    ca_p, hh_p, cb_p = stack(outs_p)
    ca_s, hh_s, cb_s = stack(outs_s)
    return (y_prompt, y_sample, ca_p, hh_p, cb_p, ca_s, hh_s, cb_s)
```

```python
import functools

import jax
import jax.numpy as jnp
from jax import lax
from jax.experimental import pallas as pl
from jax.experimental.pallas import tpu as pltpu

F32 = jnp.float32
BF16 = jnp.bfloat16

EPS = 1e-6
C_LRU = 8.0
N_RNN_BLOCKS = 16
N_KEYS = 128
PEER_HEADS = 8
TOPK = 16
N_MOD = 6

VMEM_LIMIT_BYTES = 56 * 1024 * 1024
ROW_TILE = 1024
ROUTER_ROW_TILE = 512
PEER_ROW_TILE = 1024
EXPERT_TILE = 512
MIXER_ROWS = 512
LANES = 128
SUBLANES = 8
BF16_ROWS = 16
MXU_COLS = 256


def _params(*sem):
    return pltpu.CompilerParams(dimension_semantics=sem, vmem_limit_bytes=VMEM_LIMIT_BYTES)


def _time_tile(rows, t, b):
    tt = max(1, min(t, rows // b))
    assert t % tt == 0, (t, tt)
    return tt


def _ada_kernel(c_ref, w_ref, b_ref, o_ref):
    c = c_ref[...].astype(BF16)
    w = w_ref[...].astype(BF16)
    o_ref[...] = jnp.dot(c, w, preferred_element_type=F32) + b_ref[...]


def _ada(c_all, w_ada, b_ada):
    depth, d, cols = w_ada.shape
    bc = c_all.shape[0]
    tn = 1536
    assert cols % tn == 0
    return pl.pallas_call(
        _ada_kernel,
        grid=(depth, cols // tn),
        in_specs=[
            pl.BlockSpec((bc, d), lambda l, j: (0, 0)),
            pl.BlockSpec((None, d, tn), lambda l, j: (l, 0, j)),
            pl.BlockSpec((None, 1, tn), lambda l, j: (l, 0, j)),
        ],
        out_specs=pl.BlockSpec((None, bc, tn), lambda l, j: (l, 0, j)),
        out_shape=jax.ShapeDtypeStruct((depth, bc, cols), F32),
        compiler_params=_params("parallel", "parallel"),
        name="ada_mod",
    )(c_all, w_ada, b_ada.reshape(depth, 1, cols))


def _norm_mod(x, g, sh, sc):
    var = jnp.mean(x * x, axis=-1, keepdims=True)
    xn = (x * lax.rsqrt(var + EPS)) * g
    return xn * (1.0 + sc) + sh


def _inproj_kernel(x_ref, g_ref, sh_ref, sc_ref, w_ref, o_ref, hn_ref):
    tt, b, d = x_ref.shape

    @pl.when(pl.program_id(1) == 0)
    def _():
        hn = _norm_mod(x_ref[...], g_ref[...], sh_ref[...], sc_ref[...])
        hn_ref[...] = hn.reshape(tt * b, d).astype(BF16)

    y = jnp.dot(hn_ref[...], w_ref[...], preferred_element_type=F32)
    o_ref[...] = y.reshape(tt, b, -1)


def _inproj(x, g, sh, sc, w_bf16, layer):
    t, b, d = x.shape
    cols = w_bf16.shape[2]
    tt = _time_tile(ROW_TILE, t, b)
    tn = 1536
    return pl.pallas_call(
        _inproj_kernel,
        grid=(t // tt, cols // tn),
        in_specs=[
            pl.BlockSpec((tt, b, d), lambda i, j: (i, 0, 0)),
            pl.BlockSpec((1, d), lambda i, j: (0, 0)),
            pl.BlockSpec((b, d), lambda i, j: (0, 0)),
            pl.BlockSpec((b, d), lambda i, j: (0, 0)),
            pl.BlockSpec((None, d, tn), lambda i, j: (layer, 0, j)),
        ],
        out_specs=pl.BlockSpec((tt, b, tn), lambda i, j: (i, 0, j)),
        out_shape=jax.ShapeDtypeStruct((t, b, cols), F32),
        scratch_shapes=[pltpu.VMEM((tt * b, d), BF16)],
        compiler_params=_params("parallel", "arbitrary"),
        name="in_proj",
    )(x, g, sh, sc, w_bf16)


def _mixer_kernel(is_prompt, proj_ref, x_ref, bufa_ref, h0_ref, bufb_ref, g1_ref,
                  caw_ref, cab_ref, wa_ref, ba_ref, wx_ref, bx_ref, lam_ref, cbw_ref, wout_ref,
                  xo_ref, na_ref, nh_ref, nb_ref,
                  xbuf, ubuf, h_scr, a_scr, b_scr):
    tt, bb, d = x_ref.shape
    wa_hist = bufa_ref.shape[0]
    wb_hist = bufb_ref.shape[0]
    ti = pl.program_id(1)

    @pl.when(ti == 0)
    def _():
        xbuf[0:wa_hist] = bufa_ref[...]
        ubuf[0:wb_hist] = bufb_ref[...]
        h_scr[...] = h0_ref[...]

    xa = proj_ref[:, :, 0:d]
    xbuf[wa_hist:wa_hist + tt] = xa
    xc = xbuf[0:tt] * caw_ref[0:1, :]
    for k in range(1, wa_hist + 1):
        xc = xc + xbuf[k:k + tt] * caw_ref[k:k + 1, :]
    xc = xc + cab_ref[...]
    xc2 = xc.reshape(tt * bb, d)
    xcb = xc2.astype(BF16)
    r = jax.nn.sigmoid(jnp.dot(xcb, wa_ref[...], preferred_element_type=F32) + ba_ref[...])
    ig = jax.nn.sigmoid(jnp.dot(xcb, wx_ref[...], preferred_element_type=F32) + bx_ref[...])
    nl = -lam_ref[...]
    softplus = jnp.maximum(nl, 0.0) + jnp.log1p(jnp.exp(-jnp.abs(nl)))
    log_a = (-C_LRU * r) * softplus
    a = jnp.exp(log_a)
    mult = jnp.sqrt(1.0 - a * a)
    if is_prompt:
        row = lax.broadcasted_iota(jnp.int32, (tt * bb, 1), 0)
        mult = jnp.where((row < bb) & (ti == 0), 1.0, mult)
    a_scr[...] = a.reshape(tt, bb, d)
    b_scr[...] = (mult * (ig * xc2)).reshape(tt, bb, d)

    def step(t, h):
        h = a_scr[t] * h + b_scr[t]
        b_scr[t] = h
        return h

    h = lax.fori_loop(0, tt, step, h_scr[...])
    h_scr[...] = h
    ya = b_scr[...]

    u = proj_ref[:, :, 2 * d:3 * d] * proj_ref[:, :, 3 * d:4 * d]
    ubuf[wb_hist:wb_hist + tt] = u
    uc = ubuf[0:tt] * cbw_ref[0:1, :]
    for k in range(1, wb_hist + 1):
        uc = uc + ubuf[k:k + tt] * cbw_ref[k:k + 1, :]
    yb = proj_ref[:, :, d:2 * d] * uc

    merged = (jax.nn.sigmoid(proj_ref[:, :, 4 * d:5 * d]) * ya
              + jax.nn.sigmoid(proj_ref[:, :, 5 * d:6 * d]) * yb)
    o = jnp.dot(merged.reshape(tt * bb, d).astype(BF16), wout_ref[...],
                preferred_element_type=F32)
    xo_ref[...] = x_ref[...] + g1_ref[...] * o.reshape(tt, bb, d)

    new_a = xbuf[tt:tt + wa_hist]
    new_b = ubuf[tt:tt + wb_hist]
    xbuf[0:wa_hist] = new_a
    ubuf[0:wb_hist] = new_b
    na_ref[...] = new_a
    nb_ref[...] = new_b
    nh_ref[...] = h


def _mixer(is_prompt, proj, x, bufa, h0, bufb, g1, caw, cab, wa, ba, wx, bx, lam, cbw, wout):
    t, b, d = x.shape
    cols = proj.shape[2]
    bb = min(b, 32)
    tt = _time_tile(MIXER_ROWS, t, bb)
    wa_hist, wb_hist = bufa.shape[0], bufb.shape[0]
    row = lambda a: a.reshape(1, d)
    full = lambda shape: pl.BlockSpec(shape, lambda i, j: (0,) * len(shape))
    return pl.pallas_call(
        functools.partial(_mixer_kernel, is_prompt),
        grid=(b // bb, t // tt),
        in_specs=[
            pl.BlockSpec((tt, bb, cols), lambda i, j: (j, i, 0)),
            pl.BlockSpec((tt, bb, d), lambda i, j: (j, i, 0)),
            pl.BlockSpec((wa_hist, bb, d), lambda i, j: (0, i, 0)),
            pl.BlockSpec((bb, d), lambda i, j: (i, 0)),
            pl.BlockSpec((wb_hist, bb, d), lambda i, j: (0, i, 0)),
            pl.BlockSpec((bb, d), lambda i, j: (i, 0)),
            full((wa_hist + 1, d)), full((1, d)),
            full((d, d)), full((1, d)), full((d, d)), full((1, d)), full((1, d)),
            full((wb_hist + 1, d)), full((d, d)),
        ],
        out_specs=[
            pl.BlockSpec((tt, bb, d), lambda i, j: (j, i, 0)),
            pl.BlockSpec((wa_hist, bb, d), lambda i, j: (0, i, 0)),
            pl.BlockSpec((bb, d), lambda i, j: (i, 0)),
            pl.BlockSpec((wb_hist, bb, d), lambda i, j: (0, i, 0)),
        ],
        out_shape=[
            jax.ShapeDtypeStruct((t, b, d), F32),
            jax.ShapeDtypeStruct((wa_hist, b, d), F32),
            jax.ShapeDtypeStruct((b, d), F32),
            jax.ShapeDtypeStruct((wb_hist, b, d), F32),
        ],
        scratch_shapes=[
            pltpu.VMEM((wa_hist + tt, bb, d), F32),
            pltpu.VMEM((wb_hist + tt, bb, d), F32),
            pltpu.VMEM((bb, d), F32),
            pltpu.VMEM((tt, bb, d), F32),
            pltpu.VMEM((tt, bb, d), F32),
        ],
        compiler_params=_params("parallel", "arbitrary"),
        name="mixer",
    )(proj, x, bufa, h0, bufb, g1, caw, row(cab), wa, row(ba), wx, row(bx), row(lam), cbw, wout)


def _topk_rank(s, order, k):
    rank = jnp.full(s.shape, float(k), F32)
    vals = []
    for j in range(k):
        m = jnp.max(s, axis=0, keepdims=True)
        first = jnp.min(jnp.where(s == m, order, jnp.inf), axis=0, keepdims=True)
        sel = order == first
        rank = jnp.where(sel, float(j), rank)
        s = jnp.where(sel, -jnp.inf, s)
        vals.append(m)
    return vals, rank


def _topk_values(s, k):
    vals = []
    for _ in range(k):
        m = jnp.max(s, axis=0, keepdims=True)
        s = jnp.where(s == m, -jnp.inf, s)
        vals.append(m)
    return vals


def _rank_from_values(s, vals):
    rank = jnp.full(s.shape, float(len(vals)), F32)
    for j, v in enumerate(vals):
        rank = jnp.where(s == v, float(j), rank)
    return rank


_CAND_ROWS = TOPK + 7 * 8 + 8


def _cand_flat_index(tl):
    r = lax.broadcasted_iota(jnp.int32, (_CAND_ROWS, tl), 0)
    q = r - TOPK
    mid = TOPK + (q >> 3) * TOPK + (q & 7)
    last = (r - (TOPK + 7 * 8) + 8) * TOPK
    return jnp.where(r < TOPK, r, jnp.where(r < TOPK + 7 * 8, mid, last)).astype(F32)


def _route_chunk(s0, s1, v0, rank0, v1, rank1, exact):
    tl = s0.shape[1]
    sv1 = jnp.concatenate(v1, axis=0)
    sv1_lo = sv1[0:8]
    sv0_hi = jnp.concatenate(v0[8:], axis=0)
    cand = jnp.concatenate(
        [v0[0] + sv1] + [v0[a] + sv1_lo for a in range(1, 8)] + [sv0_hi + v1[0]], axis=0)
    if exact:
        _, crank = _topk_rank(cand, _cand_flat_index(tl), TOPK)
        chosen = jnp.where(crank < float(TOPK), 1.0, 0.0)
    else:
        chosen = jnp.where(cand >= _topk_values(cand, TOPK)[-1], 1.0, 0.0)
    z = jnp.sum(chosen * jnp.exp(cand - (v0[0] + v1[0])), axis=0, keepdims=True)
    counts = [jnp.sum(chosen[0:TOPK], axis=0, keepdims=True)]
    counts += [jnp.sum(chosen[TOPK + 8 * (a - 1):TOPK + 8 * a], axis=0, keepdims=True)
               for a in range(1, 8)]
    counts += [chosen[TOPK + 7 * 8 + a:TOPK + 7 * 8 + a + 1] for a in range(8)]
    nsel = jnp.zeros(s0.shape, F32)
    for a in range(TOPK):
        nsel = nsel + jnp.where(rank0 == float(a), counts[a], 0.0)
    c = jnp.exp(s0 - v0[0]) / z
    e1 = jnp.exp(s1 - v1[0])
    return nsel, c, rank1, e1, jnp.sum(chosen, axis=0, keepdims=True)


def _dup_bf16_words(x):
    hi = pltpu.bitcast(x.astype(BF16).astype(F32), jnp.uint32)
    return hi | (hi >> 16)


def _router_kernel(x_ref, g_ref, sh_ref, sc_ref, wqt_ref, keys_ref,
                   xt_ref, nf_ref, c_ref, r1_ref, e1_ref, qt_scr):
    tt, b, d = x_ref.shape
    tm = tt * b
    hn = _norm_mod(x_ref[...], g_ref[...], sh_ref[...], sc_ref[...]).reshape(tm, d)
    xt = hn.T.astype(BF16)
    xt_ref[...] = xt
    qt_scr[...] = jnp.dot(wqt_ref[...], xt, preferred_element_type=F32)

    def head(h, carry):
        def scores(lanes):
            s = []
            for p in range(2):
                row0 = pl.multiple_of((2 * h + p) * N_KEYS, N_KEYS)
                qc = qt_scr[pl.ds(row0, N_KEYS), lanes].astype(BF16)
                s.append(jnp.dot(keys_ref[2 * h + p], qc, preferred_element_type=F32))
            return s

        def emit(lanes, s, v0, rank0, v1, rank1, exact):
            nsel, c, rank1, e1, n_chosen = _route_chunk(s[0], s[1], v0, rank0, v1, rank1, exact)
            nf_ref[h, :, lanes] = _dup_bf16_words(nsel)
            c_ref[h, :, lanes] = _dup_bf16_words(c)
            r1_ref[h, :, lanes] = pltpu.bitcast(rank1.astype(BF16), jnp.uint32)
            e1_ref[h, :, lanes] = pltpu.bitcast(e1.astype(BF16), jnp.uint32)
            return n_chosen

        tied = []
        for lo in range(0, tm, LANES):
            lanes = slice(lo, lo + LANES)
            s = scores(lanes)
            v0 = _topk_values(s[0], TOPK)
            v1 = _topk_values(s[1], TOPK)
            n_chosen = emit(lanes, s, v0, _rank_from_values(s[0], v0),
                            v1, _rank_from_values(s[1], v1), False)
            reach = [jnp.sum(jnp.where(s[p] >= v[-1], 1.0, 0.0), axis=0, keepdims=True)
                     for p, v in enumerate((v0, v1))]
            tied.append(jnp.max(jnp.maximum(jnp.maximum(reach[0], reach[1]), n_chosen))
                        > float(TOPK))

        for lo, redo in zip(range(0, tm, LANES), tied):
            @pl.when(redo)
            def _(lo=lo):
                lanes = slice(lo, lo + LANES)
                s = scores(lanes)
                key_order = lax.broadcasted_iota(jnp.int32, (N_KEYS, LANES), 0).astype(F32)
                emit(lanes, s, *_topk_rank(s[0], key_order, TOPK),
                     *_topk_rank(s[1], key_order, TOPK), True)
        return carry

    lax.fori_loop(0, PEER_HEADS, head, 0)


def _router(x, g, sh, sc, wqt_bf16, keys_bf16):
    t, b, d = x.shape
    n = t * b
    tt = _time_tile(ROUTER_ROW_TILE, t, b)
    tm = tt * b
    dq = wqt_bf16.shape[0]
    assert tm % LANES == 0
    fac = jax.ShapeDtypeStruct((PEER_HEADS, N_KEYS, n), jnp.uint32)
    fac16 = jax.ShapeDtypeStruct((PEER_HEADS, N_KEYS // 2, n), jnp.uint32)
    fac_spec = pl.BlockSpec((PEER_HEADS, N_KEYS, tm), lambda i: (0, 0, i))
    fac16_spec = pl.BlockSpec((PEER_HEADS, N_KEYS // 2, tm), lambda i: (0, 0, i))
    return pl.pallas_call(
        _router_kernel,
        grid=(t // tt,),
        in_specs=[
            pl.BlockSpec((tt, b, d), lambda i: (i, 0, 0)),
            pl.BlockSpec((1, d), lambda i: (0, 0)),
            pl.BlockSpec((b, d), lambda i: (0, 0)),
            pl.BlockSpec((b, d), lambda i: (0, 0)),
            pl.BlockSpec((dq, d), lambda i: (0, 0)),
            pl.BlockSpec((2 * PEER_HEADS, N_KEYS, N_KEYS), lambda i: (0, 0, 0)),
        ],
        out_specs=[pl.BlockSpec((d, tm), lambda i: (0, i)), fac_spec, fac_spec,
                   fac16_spec, fac16_spec],
        out_shape=[jax.ShapeDtypeStruct((d, n), BF16), fac, fac, fac16, fac16],
        scratch_shapes=[pltpu.VMEM((dq, tm), F32)],
        compiler_params=_params("parallel"),
        name="peer_router",
    )(x, g, sh, sc, wqt_bf16, keys_bf16)


def _bcast_row_words(row):
    return pltpu.bitcast(jnp.broadcast_to(row, (SUBLANES, row.shape[1])), BF16)


def _peer_kernel(xt_ref, u_ref, vt_ref, nf_ref, c_ref, r1_ref, e1_ref, x_ref, g2_ref,
                 o_ref, acc_ref, act_ref, w_ref):
    tt, b, d = x_ref.shape
    tm = tt * b
    tn = u_ref.shape[0]
    groups = tn // N_KEYS
    assert 2 * groups == SUBLANES
    k = pl.program_id(1)
    n_tiles = pl.num_programs(1) - 1

    @pl.when(k == 0)
    def _():
        acc_ref[...] = jnp.zeros_like(acc_ref)
        act_ref[...] = jnp.zeros_like(act_ref)

    def gate_chunk(prev, lo):
        lanes = slice(lo, lo + LANES)
        reps = N_KEYS // BF16_ROWS
        gates = [None] * groups
        for h in range(PEER_HEADS):
            r1 = pltpu.bitcast(r1_ref[h, :, lanes], BF16)
            e1 = pltpu.bitcast(e1_ref[h, :, lanes], BF16)
            n_rows = nf_ref[h, :, lanes]
            c_rows = c_ref[h, :, lanes]
            for ii in range(groups):
                r = prev * groups + ii
                n_b = jnp.tile(_bcast_row_words(n_rows[r:r + 1]), (reps, 1))
                c_b = jnp.tile(_bcast_row_words(c_rows[r:r + 1]), (reps, 1))
                term = c_b * jnp.where(r1 < n_b, e1, 0.0)
                gates[ii] = term if gates[ii] is None else gates[ii] + term
        for ii in range(groups):
            rows = slice(ii * N_KEYS, (ii + 1) * N_KEYS)
            w_ref[rows, lanes] = gates[ii] * act_ref[prev, rows, lanes]

    def step(cur):
        prev = 1 - cur
        for lo in range(0, tm, MXU_COLS):
            cols = slice(lo, lo + MXU_COLS)
            s = jnp.dot(u_ref[...], xt_ref[:, cols], preferred_element_type=F32)
            act = (0.5 * s) * (1.0 + lax.erf(s * (0.5 ** 0.5)))
            act_ref[cur, :, cols] = act.astype(BF16)
            for sub in range(lo, lo + MXU_COLS, LANES):
                gate_chunk(prev, sub)
            acc_ref[:, cols] += jnp.dot(vt_ref[...], w_ref[:, cols],
                                        preferred_element_type=F32)

    @pl.when(k % 2 == 0)
    def _():
        step(0)

    @pl.when(k % 2 == 1)
    def _():
        step(1)

    @pl.when(k == n_tiles)
    def _():
        o = acc_ref[...].T.reshape(tt, b, d)
        o_ref[...] = x_ref[...] + g2_ref[...] * o


def _peer(xt, u_bf16, vt_bf16, layer, nf, c, r1, e1, x, g2):
    t, b, d = x.shape
    n = t * b
    ne = u_bf16.shape[1]
    tt = _time_tile(PEER_ROW_TILE, t, b)
    tm = tt * b
    tn = EXPERT_TILE
    n_tiles = ne // tn
    groups = tn // N_KEYS
    fac_spec = pl.BlockSpec(
        (PEER_HEADS, SUBLANES, tm),
        lambda i, j: (0, jnp.clip(j - 1, 0, n_tiles - 1) * groups // SUBLANES, i))
    fac16_spec = pl.BlockSpec((PEER_HEADS, N_KEYS // 2, tm), lambda i, j: (0, 0, i))
    return pl.pallas_call(
        _peer_kernel,
        grid=(n // tm, n_tiles + 1),
        in_specs=[
            pl.BlockSpec((d, tm), lambda i, j: (0, i)),
            pl.BlockSpec((None, tn, d), lambda i, j: (layer, jnp.minimum(j, n_tiles - 1), 0)),
            pl.BlockSpec((None, d, tn),
                         lambda i, j: (layer, 0, jnp.clip(j - 1, 0, n_tiles - 1))),
            fac_spec, fac_spec, fac16_spec, fac16_spec,
            pl.BlockSpec((tt, b, d), lambda i, j: (i, 0, 0)),
            pl.BlockSpec((b, d), lambda i, j: (0, 0)),
        ],
        out_specs=pl.BlockSpec((tt, b, d), lambda i, j: (i, 0, 0)),
        out_shape=jax.ShapeDtypeStruct((t, b, d), F32),
        scratch_shapes=[
            pltpu.VMEM((d, tm), F32),
            pltpu.VMEM((2, tn, tm), BF16),
            pltpu.VMEM((tn, tm), BF16),
        ],
        compiler_params=_params("parallel", "arbitrary"),
        name="peer_experts",
    )(xt, u_bf16, vt_bf16, nf, c, r1, e1, x, g2)


def _final_norm_kernel(x_ref, g_ref, o_ref):
    x = x_ref[...]
    var = jnp.mean(x * x, axis=-1, keepdims=True)
    o_ref[...] = (x * lax.rsqrt(var + EPS)) * g_ref[...]


def _final_norm(x, g):
    t, b, d = x.shape
    tt = _time_tile(ROW_TILE, t, b)
    return pl.pallas_call(
        _final_norm_kernel,
        grid=(t // tt,),
        in_specs=[pl.BlockSpec((tt, b, d), lambda i: (i, 0, 0)),
                  pl.BlockSpec((1, d), lambda i: (0, 0))],
        out_specs=pl.BlockSpec((tt, b, d), lambda i: (i, 0, 0)),
        out_shape=jax.ShapeDtypeStruct((t, b, d), F32),
        compiler_params=_params("parallel"),
        name="final_norm",
    )(x, g)


def _block_diag(w):
    nb, bs, _ = w.shape
    eye = jnp.eye(nb, dtype=w.dtype)
    return (eye[:, None, :, None] * w[:, :, None, :]).reshape(nb * bs, nb * bs)


def _layer(x, mod, states, is_prompt, lw):
    sh1, sc1, g1, sh2, sc2, g2 = mod
    bufa, h0, bufb = states
    proj = _inproj(x, lw["norm1"], sh1, sc1, lw["w_in"], lw["layer"])
    x, na, nh, nb = _mixer(is_prompt, proj, x, bufa, h0, bufb, g1,
                           lw["conv_a_w"], lw["conv_a_b"], lw["rg_w_a"], lw["rg_b_a"],
                           lw["rg_w_x"], lw["rg_b_x"], lw["rg_lambda"], lw["conv_b_w"], lw["w_out"])
    xt, nf, c, r1, e1 = _router(x, lw["norm2"], sh2, sc2, lw["w_qt"], lw["keys"])
    x = _peer(xt, lw["u"], lw["vt"], lw["layer"], nf, c, r1, e1, x, g2)
    return x, na, nh, nb


def kernel(x_prompt, x_sample, c_prompt, c_sample, state_conv_a, state_h, state_conv_b, w_ada, b_ada, norm1, norm2, w_in, conv_a_w, conv_a_b, rg_w_a, rg_b_a, rg_w_x, rg_b_x, rg_lambda, conv_b_w, w_out, peer_w_q, peer_sub_keys, peer_u, peer_v, final_norm):
    depth = w_ada.shape[0]
    bp, _, d = x_prompt.shape
    bs = x_sample.shape[0]
    wa_hist = conv_a_w.shape[1] - 1
    wb_hist = conv_b_w.shape[1] - 1

    mods = _ada(jnp.concatenate([c_prompt, c_sample], axis=0), w_ada, b_ada)
    mods = mods.reshape(depth, bp + bs, N_MOD, d)

    xp = jnp.transpose(x_prompt, (1, 0, 2))
    xs = jnp.transpose(x_sample, (1, 0, 2))
    zeros_p = (jnp.zeros((wa_hist, bp, d), F32), jnp.zeros((bp, d), F32),
               jnp.zeros((wb_hist, bp, d), F32))
    outs_p, outs_s = [], []
    w_in_bf16 = w_in.astype(BF16)
    u_bf16 = peer_u.astype(BF16)
    vt_bf16 = jnp.swapaxes(peer_v, 1, 2).astype(BF16)
    for l in range(depth):
        lw = {
            "layer": l,
            "norm1": norm1[l].reshape(1, d), "norm2": norm2[l].reshape(1, d),
            "w_in": w_in_bf16,
            "conv_a_w": conv_a_w[l], "conv_a_b": conv_a_b[l],
            "rg_w_a": _block_diag(rg_w_a[l]).astype(BF16), "rg_b_a": rg_b_a[l],
            "rg_w_x": _block_diag(rg_w_x[l]).astype(BF16), "rg_b_x": rg_b_x[l],
            "rg_lambda": rg_lambda[l], "conv_b_w": conv_b_w[l],
            "w_out": w_out[l].astype(BF16),
            "w_qt": peer_w_q[l].T.astype(BF16),
            "keys": peer_sub_keys[l].reshape(2 * PEER_HEADS, N_KEYS, -1).astype(BF16),
            "u": u_bf16,
            "vt": vt_bf16,
        }
        mod_p = [mods[l, :bp, k] for k in range(N_MOD)]
        mod_s = [mods[l, bp:, k] for k in range(N_MOD)]
        xp, na, nh, nb = _layer(xp, mod_p, zeros_p, True, lw)
        outs_p.append((na, nh, nb))
        states_s = (jnp.transpose(state_conv_a[l], (1, 0, 2)), state_h[l],
                    jnp.transpose(state_conv_b[l], (1, 0, 2)))
        xs, na, nh, nb = _layer(xs, mod_s, states_s, False, lw)
        outs_s.append((na, nh, nb))

    fn = final_norm.reshape(1, d)
    y_prompt = jnp.transpose(_final_norm(xp, fn), (1, 0, 2))
    y_sample = jnp.transpose(_final_norm(xs, fn), (1, 0, 2))

    def stack(outs):
        ca = jnp.stack([jnp.transpose(o[0], (1, 0, 2)) for o in outs], 0)
        hh = jnp.stack([o[1] for o in outs], 0)
        cb = jnp.stack([jnp.transpose(o[2], (1, 0, 2)) for o in outs], 0)
        return ca, hh, cb

    ca_p, hh_p, cb_p = stack(outs_p)
    ca_s, hh_s, cb_s = stack(outs_s)
    return (y_prompt, y_sample, ca_p, hh_p, cb_p, ca_s, hh_s, cb_s)
```

```python
import functools

import jax
import jax.numpy as jnp
from jax import lax
from jax.experimental import pallas as pl
from jax.experimental.pallas import tpu as pltpu

F32 = jnp.float32
BF16 = jnp.bfloat16

EPS = 1e-6
C_LRU = 8.0
N_RNN_BLOCKS = 16
N_KEYS = 128
PEER_HEADS = 8
TOPK = 16
N_MOD = 6

VMEM_LIMIT_BYTES = 56 * 1024 * 1024
ROW_TILE = 1024
ROUTER_ROW_TILE = 512
PEER_ROW_TILE = 1024
EXPERT_TILE = 512
MIXER_ROWS = 512
LANES = 128
SUBLANES = 8
BF16_ROWS = 16
MXU_COLS = 256


def _params(*sem):
    return pltpu.CompilerParams(dimension_semantics=sem, vmem_limit_bytes=VMEM_LIMIT_BYTES)


def _time_tile(rows, t, b):
    tt = max(1, min(t, rows // b))
    assert t % tt == 0, (t, tt)
    return tt


def _ada_kernel(c_ref, w_ref, b_ref, o_ref):
    c = c_ref[...].astype(BF16)
    w = w_ref[...].astype(BF16)
    o_ref[...] = jnp.dot(c, w, preferred_element_type=F32) + b_ref[...]


def _ada(c_all, w_ada, b_ada):
    depth, d, cols = w_ada.shape
    bc = c_all.shape[0]
    tn = 1536
    assert cols % tn == 0
    return pl.pallas_call(
        _ada_kernel,
        grid=(depth, cols // tn),
        in_specs=[
            pl.BlockSpec((bc, d), lambda l, j: (0, 0)),
            pl.BlockSpec((None, d, tn), lambda l, j: (l, 0, j)),
            pl.BlockSpec((None, 1, tn), lambda l, j: (l, 0, j)),
        ],
        out_specs=pl.BlockSpec((None, bc, tn), lambda l, j: (l, 0, j)),
        out_shape=jax.ShapeDtypeStruct((depth, bc, cols), F32),
        compiler_params=_params("parallel", "parallel"),
        name="ada_mod",
    )(c_all, w_ada, b_ada.reshape(depth, 1, cols))


def _norm_mod(x, g, sh, sc):
    var = jnp.mean(x * x, axis=-1, keepdims=True)
    xn = (x * lax.rsqrt(var + EPS)) * g
    return xn * (1.0 + sc) + sh


def _inproj_kernel(x_ref, g_ref, sh_ref, sc_ref, w_ref, o_ref, hn_ref):
    tt, b, d = x_ref.shape

    @pl.when(pl.program_id(1) == 0)
    def _():
        hn = _norm_mod(x_ref[...], g_ref[...], sh_ref[...], sc_ref[...])
        hn_ref[...] = hn.reshape(tt * b, d).astype(BF16)

    y = jnp.dot(hn_ref[...], w_ref[...], preferred_element_type=F32)
    o_ref[...] = y.reshape(tt, b, -1)


def _inproj(x, g, sh, sc, w_bf16, layer):
    t, b, d = x.shape
    cols = w_bf16.shape[2]
    tt = _time_tile(ROW_TILE, t, b)
    tn = 1536
    return pl.pallas_call(
        _inproj_kernel,
        grid=(t // tt, cols // tn),
        in_specs=[
            pl.BlockSpec((tt, b, d), lambda i, j: (i, 0, 0)),
            pl.BlockSpec((1, d), lambda i, j: (0, 0)),
            pl.BlockSpec((b, d), lambda i, j: (0, 0)),
            pl.BlockSpec((b, d), lambda i, j: (0, 0)),
            pl.BlockSpec((None, d, tn), lambda i, j: (layer, 0, j)),
        ],
        out_specs=pl.BlockSpec((tt, b, tn), lambda i, j: (i, 0, j)),
        out_shape=jax.ShapeDtypeStruct((t, b, cols), F32),
        scratch_shapes=[pltpu.VMEM((tt * b, d), BF16)],
        compiler_params=_params("parallel", "arbitrary"),
        name="in_proj",
    )(x, g, sh, sc, w_bf16)


def _mixer_kernel(is_prompt, proj_ref, x_ref, bufa_ref, h0_ref, bufb_ref, g1_ref,
                  caw_ref, cab_ref, wa_ref, ba_ref, wx_ref, bx_ref, lam_ref, cbw_ref, wout_ref,
                  xo_ref, na_ref, nh_ref, nb_ref,
                  xbuf, ubuf, h_scr, a_scr, b_scr):
    tt, bb, d = x_ref.shape
    wa_hist = bufa_ref.shape[0]
    wb_hist = bufb_ref.shape[0]
    ti = pl.program_id(1)

    @pl.when(ti == 0)
    def _():
        xbuf[0:wa_hist] = bufa_ref[...]
        ubuf[0:wb_hist] = bufb_ref[...]
        h_scr[...] = h0_ref[...]

    xa = proj_ref[:, :, 0:d]
    xbuf[wa_hist:wa_hist + tt] = xa
    xc = xbuf[0:tt] * caw_ref[0:1, :]
    for k in range(1, wa_hist + 1):
        xc = xc + xbuf[k:k + tt] * caw_ref[k:k + 1, :]
    xc = xc + cab_ref[...]
    xc2 = xc.reshape(tt * bb, d)
    xcb = xc2.astype(BF16)
    r = jax.nn.sigmoid(jnp.dot(xcb, wa_ref[...], preferred_element_type=F32) + ba_ref[...])
    ig = jax.nn.sigmoid(jnp.dot(xcb, wx_ref[...], preferred_element_type=F32) + bx_ref[...])
    nl = -lam_ref[...]
    softplus = jnp.maximum(nl, 0.0) + jnp.log1p(jnp.exp(-jnp.abs(nl)))
    log_a = (-C_LRU * r) * softplus
    a = jnp.exp(log_a)
    mult = jnp.sqrt(1.0 - a * a)
    if is_prompt:
        row = lax.broadcasted_iota(jnp.int32, (tt * bb, 1), 0)
        mult = jnp.where((row < bb) & (ti == 0), 1.0, mult)
    a_scr[...] = a.reshape(tt, bb, d)
    b_scr[...] = (mult * (ig * xc2)).reshape(tt, bb, d)

    def step(t, h):
        h = a_scr[t] * h + b_scr[t]
        b_scr[t] = h
        return h

    h = lax.fori_loop(0, tt, step, h_scr[...])
    h_scr[...] = h
    ya = b_scr[...]

    u = proj_ref[:, :, 2 * d:3 * d] * proj_ref[:, :, 3 * d:4 * d]
    ubuf[wb_hist:wb_hist + tt] = u
    uc = ubuf[0:tt] * cbw_ref[0:1, :]
    for k in range(1, wb_hist + 1):
        uc = uc + ubuf[k:k + tt] * cbw_ref[k:k + 1, :]
    yb = proj_ref[:, :, d:2 * d] * uc

    merged = (jax.nn.sigmoid(proj_ref[:, :, 4 * d:5 * d]) * ya
              + jax.nn.sigmoid(proj_ref[:, :, 5 * d:6 * d]) * yb)
    o = jnp.dot(merged.reshape(tt * bb, d).astype(BF16), wout_ref[...],
                preferred_element_type=F32)
    xo_ref[...] = x_ref[...] + g1_ref[...] * o.reshape(tt, bb, d)

    new_a = xbuf[tt:tt + wa_hist]
    new_b = ubuf[tt:tt + wb_hist]
    xbuf[0:wa_hist] = new_a
    ubuf[0:wb_hist] = new_b
    na_ref[...] = new_a
    nb_ref[...] = new_b
    nh_ref[...] = h


def _mixer(is_prompt, proj, x, bufa, h0, bufb, g1, caw, cab, wa, ba, wx, bx, lam, cbw, wout):
    t, b, d = x.shape
    cols = proj.shape[2]
    bb = min(b, 32)
    tt = _time_tile(MIXER_ROWS, t, bb)
    wa_hist, wb_hist = bufa.shape[0], bufb.shape[0]
    row = lambda a: a.reshape(1, d)
    full = lambda shape: pl.BlockSpec(shape, lambda i, j: (0,) * len(shape))
    return pl.pallas_call(
        functools.partial(_mixer_kernel, is_prompt),
        grid=(b // bb, t // tt),
        in_specs=[
            pl.BlockSpec((tt, bb, cols), lambda i, j: (j, i, 0)),
            pl.BlockSpec((tt, bb, d), lambda i, j: (j, i, 0)),
            pl.BlockSpec((wa_hist, bb, d), lambda i, j: (0, i, 0)),
            pl.BlockSpec((bb, d), lambda i, j: (i, 0)),
            pl.BlockSpec((wb_hist, bb, d), lambda i, j: (0, i, 0)),
            pl.BlockSpec((bb, d), lambda i, j: (i, 0)),
            full((wa_hist + 1, d)), full((1, d)),
            full((d, d)), full((1, d)), full((d, d)), full((1, d)), full((1, d)),
            full((wb_hist + 1, d)), full((d, d)),
        ],
        out_specs=[
            pl.BlockSpec((tt, bb, d), lambda i, j: (j, i, 0)),
            pl.BlockSpec((wa_hist, bb, d), lambda i, j: (0, i, 0)),
            pl.BlockSpec((bb, d), lambda i, j: (i, 0)),
            pl.BlockSpec((wb_hist, bb, d), lambda i, j: (0, i, 0)),
        ],
        out_shape=[
            jax.ShapeDtypeStruct((t, b, d), F32),
            jax.ShapeDtypeStruct((wa_hist, b, d), F32),
            jax.ShapeDtypeStruct((b, d), F32),
            jax.ShapeDtypeStruct((wb_hist, b, d), F32),
        ],
        scratch_shapes=[
            pltpu.VMEM((wa_hist + tt, bb, d), F32),
            pltpu.VMEM((wb_hist + tt, bb, d), F32),
            pltpu.VMEM((bb, d), F32),
            pltpu.VMEM((tt, bb, d), F32),
            pltpu.VMEM((tt, bb, d), F32),
        ],
        compiler_params=_params("parallel", "arbitrary"),
        name="mixer",
    )(proj, x, bufa, h0, bufb, g1, caw, row(cab), wa, row(ba), wx, row(bx), row(lam), cbw, wout)


def _topk_rank(s, order, k):
    rank = jnp.full(s.shape, float(k), F32)
    vals = []
    for j in range(k):
        m = jnp.max(s, axis=0, keepdims=True)
        first = jnp.min(jnp.where(s == m, order, jnp.inf), axis=0, keepdims=True)
        sel = order == first
        rank = jnp.where(sel, float(j), rank)
        s = jnp.where(sel, -jnp.inf, s)
        vals.append(m)
    return vals, rank


def _topk_values(s, k):
    vals = []
    for _ in range(k):
        m = jnp.max(s, axis=0, keepdims=True)
        s = jnp.where(s == m, -jnp.inf, s)
        vals.append(m)
    return vals


def _sorted_top16(s):
    n = TOPK
    assert s.shape[0] == n * SUBLANES
    t = [s[v * SUBLANES:(v + 1) * SUBLANES] for v in range(n)]

    def exchange(i, j, descending):
        hi, lo = jnp.maximum(t[i], t[j]), jnp.minimum(t[i], t[j])
        t[i], t[j] = (hi, lo) if descending else (lo, hi)

    k = 2
    while k <= n:
        j = k // 2
        while j >= 1:
            for i in range(n):
                if i & j == 0:
                    exchange(i, i | j, (i & k) == 0 or k == n)
            j //= 2
        k *= 2
    dist = 1
    while dist < SUBLANES:
        other = [pltpu.roll(x, SUBLANES - dist, 0) for x in t]
        t = [jnp.maximum(t[p], other[n - 1 - p]) for p in range(n)]
        j = n // 2
        while j >= 1:
            for i in range(n):
                if i & j == 0:
                    exchange(i, i | j, True)
            j //= 2
        dist *= 2
    return [x[0:1] for x in t]


def _rank_from_values(s, vals):
    rank = jnp.full(s.shape, float(len(vals)), F32)
    for j, v in enumerate(vals):
        rank = jnp.where(s == v, float(j), rank)
    return rank


_CAND_ROWS = TOPK + 7 * 8 + 8


def _cand_flat_index(tl):
    r = lax.broadcasted_iota(jnp.int32, (_CAND_ROWS, tl), 0)
    q = r - TOPK
    mid = TOPK + (q >> 3) * TOPK + (q & 7)
    last = (r - (TOPK + 7 * 8) + 8) * TOPK
    return jnp.where(r < TOPK, r, jnp.where(r < TOPK + 7 * 8, mid, last)).astype(F32)


def _route_chunk(s0, s1, v0, rank0, v1, rank1, exact):
    tl = s0.shape[1]
    sv1 = jnp.concatenate(v1, axis=0)
    sv1_lo = sv1[0:8]
    sv0_hi = jnp.concatenate(v0[8:], axis=0)
    cand = jnp.concatenate(
        [v0[0] + sv1] + [v0[a] + sv1_lo for a in range(1, 8)] + [sv0_hi + v1[0]], axis=0)
    if exact:
        _, crank = _topk_rank(cand, _cand_flat_index(tl), TOPK)
        chosen = jnp.where(crank < float(TOPK), 1.0, 0.0)
    else:
        chosen = jnp.where(cand >= _topk_values(cand, TOPK)[-1], 1.0, 0.0)
    z = jnp.sum(chosen * jnp.exp(cand - (v0[0] + v1[0])), axis=0, keepdims=True)
    if exact:
        counts = [jnp.sum(chosen[0:TOPK], axis=0, keepdims=True)]
        counts += [jnp.sum(chosen[TOPK + 8 * (a - 1):TOPK + 8 * a], axis=0, keepdims=True)
                   for a in range(1, 8)]
        counts += [chosen[TOPK + 7 * 8 + a:TOPK + 7 * 8 + a + 1] for a in range(8)]
        nsel = jnp.zeros(s0.shape, F32)
        for a in range(TOPK):
            nsel = nsel + jnp.where(rank0 == float(a), counts[a], 0.0)
    else:
        m_lo = chosen[0:8]
        for a in range(1, 8):
            m_lo = m_lo + chosen[TOPK + 8 * (a - 1):TOPK + 8 * a]
        tail = jnp.sum(chosen[TOPK + 7 * 8:], axis=0, keepdims=True)
        sub = lax.broadcasted_iota(jnp.int32, m_lo.shape, 0)
        m_lo = m_lo + jnp.where(sub == 0, tail, 0.0)
        m_hi = jnp.sum(chosen[8:TOPK], axis=0, keepdims=True)
        thresh = jnp.full(m_lo.shape, jnp.inf, F32)
        for a in range(TOPK):
            thresh = jnp.where(m_lo == float(a + 1), v0[a], thresh)
        nsel = jnp.where(s0 >= v0[0], m_hi, 0.0)
        for b in range(8):
            nsel = nsel + jnp.where(s0 >= thresh[b:b + 1], 1.0, 0.0)
    c = jnp.exp(s0 - v0[0]) / z
    e1 = jnp.exp(s1 - v1[0])
    return nsel, c, rank1, e1, jnp.sum(chosen, axis=0, keepdims=True)


def _dup_bf16_words(x):
    hi = pltpu.bitcast(x.astype(BF16).astype(F32), jnp.uint32)
    return hi | (hi >> 16)


def _router_kernel(x_ref, g_ref, sh_ref, sc_ref, wqt_ref, keys_ref,
                   xt_ref, nf_ref, c_ref, r1_ref, e1_ref, qt_scr):
    tt, b, d = x_ref.shape
    tm = tt * b
    hn = _norm_mod(x_ref[...], g_ref[...], sh_ref[...], sc_ref[...]).reshape(tm, d)
    xt = hn.T.astype(BF16)
    xt_ref[...] = xt
    qt_scr[...] = jnp.dot(wqt_ref[...], xt, preferred_element_type=F32)

    def head(h, carry):
        def scores(lanes):
            s = []
            for p in range(2):
                row0 = pl.multiple_of((2 * h + p) * N_KEYS, N_KEYS)
                qc = qt_scr[pl.ds(row0, N_KEYS), lanes].astype(BF16)
                s.append(jnp.dot(keys_ref[2 * h + p], qc, preferred_element_type=F32))
            return s

        def emit(lanes, s, v0, rank0, v1, rank1, exact):
            nsel, c, rank1, e1, n_chosen = _route_chunk(s[0], s[1], v0, rank0, v1, rank1, exact)
            nf_ref[h, :, lanes] = _dup_bf16_words(nsel)
            c_ref[h, :, lanes] = _dup_bf16_words(c)
            r1_ref[h, :, lanes] = pltpu.bitcast(rank1.astype(BF16), jnp.uint32)
            e1_ref[h, :, lanes] = pltpu.bitcast(e1.astype(BF16), jnp.uint32)
            return n_chosen

        tied = []
        for lo in range(0, tm, LANES):
            lanes = slice(lo, lo + LANES)
            s = scores(lanes)
            v0 = _sorted_top16(s[0])
            v1 = _sorted_top16(s[1])
            n_chosen = emit(lanes, s, v0, None, v1, _rank_from_values(s[1], v1), False)
            extra = n_chosen
            for p, v in enumerate((v0, v1)):
                reach = jnp.sum(jnp.where(s[p] >= v[-1], 1.0, 0.0), axis=0, keepdims=True)
                for a in range(TOPK - 1):
                    reach = reach + jnp.where(v[a] == v[a + 1], 1.0, 0.0)
                extra = jnp.maximum(extra, reach)
            tied.append(jnp.max(extra) > float(TOPK))

        for lo, redo in zip(range(0, tm, LANES), tied):
            @pl.when(redo)
            def _(lo=lo):
                lanes = slice(lo, lo + LANES)
                s = scores(lanes)
                key_order = lax.broadcasted_iota(jnp.int32, (N_KEYS, LANES), 0).astype(F32)
                emit(lanes, s, *_topk_rank(s[0], key_order, TOPK),
                     *_topk_rank(s[1], key_order, TOPK), True)
        return carry

    lax.fori_loop(0, PEER_HEADS, head, 0)


def _router(x, g, sh, sc, wqt_bf16, keys_bf16):
    t, b, d = x.shape
    n = t * b
    tt = _time_tile(ROUTER_ROW_TILE, t, b)
    tm = tt * b
    dq = wqt_bf16.shape[0]
    assert tm % LANES == 0
    fac = jax.ShapeDtypeStruct((PEER_HEADS, N_KEYS, n), jnp.uint32)
    fac16 = jax.ShapeDtypeStruct((PEER_HEADS, N_KEYS // 2, n), jnp.uint32)
    fac_spec = pl.BlockSpec((PEER_HEADS, N_KEYS, tm), lambda i: (0, 0, i))
    fac16_spec = pl.BlockSpec((PEER_HEADS, N_KEYS // 2, tm), lambda i: (0, 0, i))
    return pl.pallas_call(
        _router_kernel,
        grid=(t // tt,),
        in_specs=[
            pl.BlockSpec((tt, b, d), lambda i: (i, 0, 0)),
            pl.BlockSpec((1, d), lambda i: (0, 0)),
            pl.BlockSpec((b, d), lambda i: (0, 0)),
            pl.BlockSpec((b, d), lambda i: (0, 0)),
            pl.BlockSpec((dq, d), lambda i: (0, 0)),
            pl.BlockSpec((2 * PEER_HEADS, N_KEYS, N_KEYS), lambda i: (0, 0, 0)),
        ],
        out_specs=[pl.BlockSpec((d, tm), lambda i: (0, i)), fac_spec, fac_spec,
                   fac16_spec, fac16_spec],
        out_shape=[jax.ShapeDtypeStruct((d, n), BF16), fac, fac, fac16, fac16],
        scratch_shapes=[pltpu.VMEM((dq, tm), F32)],
        compiler_params=_params("parallel"),
        name="peer_router",
    )(x, g, sh, sc, wqt_bf16, keys_bf16)


def _bcast_row_words(row):
    return pltpu.bitcast(jnp.broadcast_to(row, (SUBLANES, row.shape[1])), BF16)


def _peer_kernel(xt_ref, u_ref, vt_ref, nf_ref, c_ref, r1_ref, e1_ref, x_ref, g2_ref,
                 o_ref, acc_ref, act_ref, w_ref):
    tt, b, d = x_ref.shape
    tm = tt * b
    tn = u_ref.shape[0]
    groups = tn // N_KEYS
    assert 2 * groups == SUBLANES
    k = pl.program_id(1)
    n_tiles = pl.num_programs(1) - 1

    @pl.when(k == 0)
    def _():
        acc_ref[...] = jnp.zeros_like(acc_ref)
        act_ref[...] = jnp.zeros_like(act_ref)

    def gate_chunk(prev, lo):
        lanes = slice(lo, lo + LANES)
        reps = N_KEYS // BF16_ROWS
        gates = [None] * groups
        for h in range(PEER_HEADS):
            r1 = pltpu.bitcast(r1_ref[h, :, lanes], BF16)
            e1 = pltpu.bitcast(e1_ref[h, :, lanes], BF16)
            n_rows = nf_ref[h, :, lanes]
            c_rows = c_ref[h, :, lanes]
            for ii in range(groups):
                r = prev * groups + ii
                n_b = jnp.tile(_bcast_row_words(n_rows[r:r + 1]), (reps, 1))
                c_b = jnp.tile(_bcast_row_words(c_rows[r:r + 1]), (reps, 1))
                term = c_b * jnp.where(r1 < n_b, e1, 0.0)
                gates[ii] = term if gates[ii] is None else gates[ii] + term
        for ii in range(groups):
            rows = slice(ii * N_KEYS, (ii + 1) * N_KEYS)
            w_ref[rows, lanes] = gates[ii] * act_ref[prev, rows, lanes]

    def step(cur):
        prev = 1 - cur
        for lo in range(0, tm, MXU_COLS):
            cols = slice(lo, lo + MXU_COLS)
            s = jnp.dot(u_ref[...], xt_ref[:, cols], preferred_element_type=F32)
            act = (0.5 * s) * (1.0 + lax.erf(s * (0.5 ** 0.5)))
            act_ref[cur, :, cols] = act.astype(BF16)
            for sub in range(lo, lo + MXU_COLS, LANES):
                gate_chunk(prev, sub)
            acc_ref[:, cols] += jnp.dot(vt_ref[...], w_ref[:, cols],
                                        preferred_element_type=F32)

    @pl.when(k % 2 == 0)
    def _():
        step(0)

    @pl.when(k % 2 == 1)
    def _():
        step(1)

    @pl.when(k == n_tiles)
    def _():
        o = acc_ref[...].T.reshape(tt, b, d)
        o_ref[...] = x_ref[...] + g2_ref[...] * o


def _peer(xt, u_bf16, vt_bf16, layer, nf, c, r1, e1, x, g2):
    t, b, d = x.shape
    n = t * b
    ne = u_bf16.shape[1]
    tt = _time_tile(PEER_ROW_TILE, t, b)
    tm = tt * b
    tn = EXPERT_TILE
    n_tiles = ne // tn
    groups = tn // N_KEYS
    fac_spec = pl.BlockSpec(
        (PEER_HEADS, SUBLANES, tm),
        lambda i, j: (0, jnp.clip(j - 1, 0, n_tiles - 1) * groups // SUBLANES, i))
    fac16_spec = pl.BlockSpec((PEER_HEADS, N_KEYS // 2, tm), lambda i, j: (0, 0, i))
    return pl.pallas_call(
        _peer_kernel,
        grid=(n // tm, n_tiles + 1),
        in_specs=[
            pl.BlockSpec((d, tm), lambda i, j: (0, i)),
            pl.BlockSpec((None, tn, d), lambda i, j: (layer, jnp.minimum(j, n_tiles - 1), 0)),
            pl.BlockSpec((None, d, tn),
                         lambda i, j: (layer, 0, jnp.clip(j - 1, 0, n_tiles - 1))),
            fac_spec, fac_spec, fac16_spec, fac16_spec,
            pl.BlockSpec((tt, b, d), lambda i, j: (i, 0, 0)),
            pl.BlockSpec((b, d), lambda i, j: (0, 0)),
        ],
        out_specs=pl.BlockSpec((tt, b, d), lambda i, j: (i, 0, 0)),
        out_shape=jax.ShapeDtypeStruct((t, b, d), F32),
        scratch_shapes=[
            pltpu.VMEM((d, tm), F32),
            pltpu.VMEM((2, tn, tm), BF16),
            pltpu.VMEM((tn, tm), BF16),
        ],
        compiler_params=_params("parallel", "arbitrary"),
        name="peer_experts",
    )(xt, u_bf16, vt_bf16, nf, c, r1, e1, x, g2)


def _final_norm_kernel(x_ref, g_ref, o_ref):
    x = x_ref[...]
    var = jnp.mean(x * x, axis=-1, keepdims=True)
    o_ref[...] = (x * lax.rsqrt(var + EPS)) * g_ref[...]


def _final_norm(x, g):
    t, b, d = x.shape
    tt = _time_tile(ROW_TILE, t, b)
    return pl.pallas_call(
        _final_norm_kernel,
        grid=(t // tt,),
        in_specs=[pl.BlockSpec((tt, b, d), lambda i: (i, 0, 0)),
                  pl.BlockSpec((1, d), lambda i: (0, 0))],
        out_specs=pl.BlockSpec((tt, b, d), lambda i: (i, 0, 0)),
        out_shape=jax.ShapeDtypeStruct((t, b, d), F32),
        compiler_params=_params("parallel"),
        name="final_norm",
    )(x, g)


def _block_diag(w):
    nb, bs, _ = w.shape
    eye = jnp.eye(nb, dtype=w.dtype)
    return (eye[:, None, :, None] * w[:, :, None, :]).reshape(nb * bs, nb * bs)


def _layer(x, mod, states, is_prompt, lw):
    sh1, sc1, g1, sh2, sc2, g2 = mod
    bufa, h0, bufb = states
    proj = _inproj(x, lw["norm1"], sh1, sc1, lw["w_in"], lw["layer"])
    x, na, nh, nb = _mixer(is_prompt, proj, x, bufa, h0, bufb, g1,
                           lw["conv_a_w"], lw["conv_a_b"], lw["rg_w_a"], lw["rg_b_a"],
                           lw["rg_w_x"], lw["rg_b_x"], lw["rg_lambda"], lw["conv_b_w"], lw["w_out"])
    xt, nf, c, r1, e1 = _router(x, lw["norm2"], sh2, sc2, lw["w_qt"], lw["keys"])
    x = _peer(xt, lw["u"], lw["vt"], lw["layer"], nf, c, r1, e1, x, g2)
    return x, na, nh, nb


def kernel(x_prompt, x_sample, c_prompt, c_sample, state_conv_a, state_h, state_conv_b, w_ada, b_ada, norm1, norm2, w_in, conv_a_w, conv_a_b, rg_w_a, rg_b_a, rg_w_x, rg_b_x, rg_lambda, conv_b_w, w_out, peer_w_q, peer_sub_keys, peer_u, peer_v, final_norm):
    depth = w_ada.shape[0]
    bp, _, d = x_prompt.shape
    bs = x_sample.shape[0]
    wa_hist = conv_a_w.shape[1] - 1
    wb_hist = conv_b_w.shape[1] - 1

    mods = _ada(jnp.concatenate([c_prompt, c_sample], axis=0), w_ada, b_ada)
    mods = mods.reshape(depth, bp + bs, N_MOD, d)

    xp = jnp.transpose(x_prompt, (1, 0, 2))
    xs = jnp.transpose(x_sample, (1, 0, 2))
    zeros_p = (jnp.zeros((wa_hist, bp, d), F32), jnp.zeros((bp, d), F32),
               jnp.zeros((wb_hist, bp, d), F32))
    outs_p, outs_s = [], []
    w_in_bf16 = w_in.astype(BF16)
    u_bf16 = peer_u.astype(BF16)
    vt_bf16 = jnp.swapaxes(peer_v, 1, 2).astype(BF16)
    for l in range(depth):
        lw = {
            "layer": l,
            "norm1": norm1[l].reshape(1, d), "norm2": norm2[l].reshape(1, d),
            "w_in": w_in_bf16,
            "conv_a_w": conv_a_w[l], "conv_a_b": conv_a_b[l],
            "rg_w_a": _block_diag(rg_w_a[l]).astype(BF16), "rg_b_a": rg_b_a[l],
            "rg_w_x": _block_diag(rg_w_x[l]).astype(BF16), "rg_b_x": rg_b_x[l],
            "rg_lambda": rg_lambda[l], "conv_b_w": conv_b_w[l],
            "w_out": w_out[l].astype(BF16),
            "w_qt": peer_w_q[l].T.astype(BF16),
            "keys": peer_sub_keys[l].reshape(2 * PEER_HEADS, N_KEYS, -1).astype(BF16),
            "u": u_bf16,
            "vt": vt_bf16,
        }
        mod_p = [mods[l, :bp, k] for k in range(N_MOD)]
        mod_s = [mods[l, bp:, k] for k in range(N_MOD)]
        xp, na, nh, nb = _layer(xp, mod_p, zeros_p, True, lw)
        outs_p.append((na, nh, nb))
        states_s = (jnp.transpose(state_conv_a[l], (1, 0, 2)), state_h[l],
                    jnp.transpose(state_conv_b[l], (1, 0, 2)))
        xs, na, nh, nb = _layer(xs, mod_s, states_s, False, lw)
        outs_s.append((na, nh, nb))

    fn = final_norm.reshape(1, d)
    y_prompt = jnp.transpose(_final_norm(xp, fn), (1, 0, 2))
    y_sample = jnp.transpose(_final_norm(xs, fn), (1, 0, 2))

    def stack(outs):
        ca = jnp.stack([jnp.transpose(o[0], (1, 0, 2)) for o in outs], 0)
        hh = jnp.stack([o[1] for o in outs], 0)
        cb = jnp.stack([jnp.transpose(o[2], (1, 0, 2)) for o in outs], 0)
        return ca, hh, cb

    ca_p, hh_p, cb_p = stack(outs_p)
    ca_s, hh_s, cb_s = stack(outs_s)
    return (y_prompt, y_sample, ca_p, hh_p, cb_p, ca_s, hh_s, cb_s)
```

```python
import functools

import jax
import jax.numpy as jnp
from jax import lax
from jax.experimental import pallas as pl
from jax.experimental.pallas import tpu as pltpu

F32 = jnp.float32
BF16 = jnp.bfloat16

EPS = 1e-6
C_LRU = 8.0
N_RNN_BLOCKS = 16
N_KEYS = 128
PEER_HEADS = 8
TOPK = 16
N_MOD = 6

VMEM_LIMIT_BYTES = 56 * 1024 * 1024
ROW_TILE = 1024
ROUTER_ROW_TILE = 512
PEER_ROW_TILE = 1024
EXPERT_TILE = 512
MIXER_ROWS = 512
LANES = 128
SUBLANES = 8
BF16_ROWS = 16
MXU_COLS = 256


def _params(*sem):
    return pltpu.CompilerParams(dimension_semantics=sem, vmem_limit_bytes=VMEM_LIMIT_BYTES)


def _time_tile(rows, t, b):
    tt = max(1, min(t, rows // b))
    assert t % tt == 0, (t, tt)
    return tt


def _ada_kernel(c_ref, w_ref, b_ref, o_ref):
    c = c_ref[...].astype(BF16)
    w = w_ref[...].astype(BF16)
    o_ref[...] = jnp.dot(c, w, preferred_element_type=F32) + b_ref[...]


def _ada(c_all, w_ada, b_ada):
    depth, d, cols = w_ada.shape
    bc = c_all.shape[0]
    tn = 1536
    assert cols % tn == 0
    return pl.pallas_call(
        _ada_kernel,
        grid=(depth, cols // tn),
        in_specs=[
            pl.BlockSpec((bc, d), lambda l, j: (0, 0)),
            pl.BlockSpec((None, d, tn), lambda l, j: (l, 0, j)),
            pl.BlockSpec((None, 1, tn), lambda l, j: (l, 0, j)),
        ],
        out_specs=pl.BlockSpec((None, bc, tn), lambda l, j: (l, 0, j)),
        out_shape=jax.ShapeDtypeStruct((depth, bc, cols), F32),
        compiler_params=_params("parallel", "parallel"),
        name="ada_mod",
    )(c_all, w_ada, b_ada.reshape(depth, 1, cols))


def _norm_mod(x, g, sh, sc):
    var = jnp.mean(x * x, axis=-1, keepdims=True)
    xn = (x * lax.rsqrt(var + EPS)) * g
    return xn * (1.0 + sc) + sh


def _inproj_kernel(x_ref, g_ref, sh_ref, sc_ref, w_ref, o_ref, hn_ref):
    tt, b, d = x_ref.shape

    @pl.when(pl.program_id(1) == 0)
    def _():
        hn = _norm_mod(x_ref[...], g_ref[...], sh_ref[...], sc_ref[...])
        hn_ref[...] = hn.reshape(tt * b, d).astype(BF16)

    y = jnp.dot(hn_ref[...], w_ref[...], preferred_element_type=F32)
    o_ref[...] = y.reshape(tt, b, -1)


def _inproj(x, g, sh, sc, w_bf16, layer):
    t, b, d = x.shape
    cols = w_bf16.shape[2]
    tt = _time_tile(ROW_TILE, t, b)
    tn = 1536
    return pl.pallas_call(
        _inproj_kernel,
        grid=(t // tt, cols // tn),
        in_specs=[
            pl.BlockSpec((tt, b, d), lambda i, j: (i, 0, 0)),
            pl.BlockSpec((1, d), lambda i, j: (0, 0)),
            pl.BlockSpec((b, d), lambda i, j: (0, 0)),
            pl.BlockSpec((b, d), lambda i, j: (0, 0)),
            pl.BlockSpec((None, d, tn), lambda i, j: (layer, 0, j)),
        ],
        out_specs=pl.BlockSpec((tt, b, tn), lambda i, j: (i, 0, j)),
        out_shape=jax.ShapeDtypeStruct((t, b, cols), F32),
        scratch_shapes=[pltpu.VMEM((tt * b, d), BF16)],
        compiler_params=_params("parallel", "arbitrary"),
        name="in_proj",
    )(x, g, sh, sc, w_bf16)


def _mixer_kernel(is_prompt, proj_ref, x_ref, bufa_ref, h0_ref, bufb_ref, g1_ref,
                  caw_ref, cab_ref, wa_ref, ba_ref, wx_ref, bx_ref, lam_ref, cbw_ref, wout_ref,
                  xo_ref, na_ref, nh_ref, nb_ref,
                  xbuf, ubuf, h_scr, a_scr, b_scr):
    tt, bb, d = x_ref.shape
    wa_hist = bufa_ref.shape[0]
    wb_hist = bufb_ref.shape[0]
    ti = pl.program_id(1)

    @pl.when(ti == 0)
    def _():
        xbuf[0:wa_hist] = bufa_ref[...]
        ubuf[0:wb_hist] = bufb_ref[...]
        h_scr[...] = h0_ref[...]

    xa = proj_ref[:, :, 0:d]
    xbuf[wa_hist:wa_hist + tt] = xa
    xc = xbuf[0:tt] * caw_ref[0:1, :]
    for k in range(1, wa_hist + 1):
        xc = xc + xbuf[k:k + tt] * caw_ref[k:k + 1, :]
    xc = xc + cab_ref[...]
    xc2 = xc.reshape(tt * bb, d)
    xcb = xc2.astype(BF16)
    r = jax.nn.sigmoid(jnp.dot(xcb, wa_ref[...], preferred_element_type=F32) + ba_ref[...])
    ig = jax.nn.sigmoid(jnp.dot(xcb, wx_ref[...], preferred_element_type=F32) + bx_ref[...])
    nl = -lam_ref[...]
    softplus = jnp.maximum(nl, 0.0) + jnp.log1p(jnp.exp(-jnp.abs(nl)))
    log_a = (-C_LRU * r) * softplus
    a = jnp.exp(log_a)
    mult = jnp.sqrt(1.0 - a * a)
    if is_prompt:
        row = lax.broadcasted_iota(jnp.int32, (tt * bb, 1), 0)
        mult = jnp.where((row < bb) & (ti == 0), 1.0, mult)
    a_scr[...] = a.reshape(tt, bb, d)
    b_scr[...] = (mult * (ig * xc2)).reshape(tt, bb, d)

    def step(t, h):
        h = a_scr[t] * h + b_scr[t]
        b_scr[t] = h
        return h

    h = lax.fori_loop(0, tt, step, h_scr[...])
    h_scr[...] = h
    ya = b_scr[...]

    u = proj_ref[:, :, 2 * d:3 * d] * proj_ref[:, :, 3 * d:4 * d]
    ubuf[wb_hist:wb_hist + tt] = u
    uc = ubuf[0:tt] * cbw_ref[0:1, :]
    for k in range(1, wb_hist + 1):
        uc = uc + ubuf[k:k + tt] * cbw_ref[k:k + 1, :]
    yb = proj_ref[:, :, d:2 * d] * uc

    merged = (jax.nn.sigmoid(proj_ref[:, :, 4 * d:5 * d]) * ya
              + jax.nn.sigmoid(proj_ref[:, :, 5 * d:6 * d]) * yb)
    o = jnp.dot(merged.reshape(tt * bb, d).astype(BF16), wout_ref[...],
                preferred_element_type=F32)
    xo_ref[...] = x_ref[...] + g1_ref[...] * o.reshape(tt, bb, d)

    new_a = xbuf[tt:tt + wa_hist]
    new_b = ubuf[tt:tt + wb_hist]
    xbuf[0:wa_hist] = new_a
    ubuf[0:wb_hist] = new_b
    na_ref[...] = new_a
    nb_ref[...] = new_b
    nh_ref[...] = h


def _mixer(is_prompt, proj, x, bufa, h0, bufb, g1, caw, cab, wa, ba, wx, bx, lam, cbw, wout):
    t, b, d = x.shape
    cols = proj.shape[2]
    bb = min(b, 32)
    tt = _time_tile(MIXER_ROWS, t, bb)
    wa_hist, wb_hist = bufa.shape[0], bufb.shape[0]
    row = lambda a: a.reshape(1, d)
    full = lambda shape: pl.BlockSpec(shape, lambda i, j: (0,) * len(shape))
    return pl.pallas_call(
        functools.partial(_mixer_kernel, is_prompt),
        grid=(b // bb, t // tt),
        in_specs=[
            pl.BlockSpec((tt, bb, cols), lambda i, j: (j, i, 0)),
            pl.BlockSpec((tt, bb, d), lambda i, j: (j, i, 0)),
            pl.BlockSpec((wa_hist, bb, d), lambda i, j: (0, i, 0)),
            pl.BlockSpec((bb, d), lambda i, j: (i, 0)),
            pl.BlockSpec((wb_hist, bb, d), lambda i, j: (0, i, 0)),
            pl.BlockSpec((bb, d), lambda i, j: (i, 0)),
            full((wa_hist + 1, d)), full((1, d)),
            full((d, d)), full((1, d)), full((d, d)), full((1, d)), full((1, d)),
            full((wb_hist + 1, d)), full((d, d)),
        ],
        out_specs=[
            pl.BlockSpec((tt, bb, d), lambda i, j: (j, i, 0)),
            pl.BlockSpec((wa_hist, bb, d), lambda i, j: (0, i, 0)),
            pl.BlockSpec((bb, d), lambda i, j: (i, 0)),
            pl.BlockSpec((wb_hist, bb, d), lambda i, j: (0, i, 0)),
        ],
        out_shape=[
            jax.ShapeDtypeStruct((t, b, d), F32),
            jax.ShapeDtypeStruct((wa_hist, b, d), F32),
            jax.ShapeDtypeStruct((b, d), F32),
            jax.ShapeDtypeStruct((wb_hist, b, d), F32),
        ],
        scratch_shapes=[
            pltpu.VMEM((wa_hist + tt, bb, d), F32),
            pltpu.VMEM((wb_hist + tt, bb, d), F32),
            pltpu.VMEM((bb, d), F32),
            pltpu.VMEM((tt, bb, d), F32),
            pltpu.VMEM((tt, bb, d), F32),
        ],
        compiler_params=_params("parallel", "arbitrary"),
        name="mixer",
    )(proj, x, bufa, h0, bufb, g1, caw, row(cab), wa, row(ba), wx, row(bx), row(lam), cbw, wout)


def _topk_rank(s, order, k):
    rank = jnp.full(s.shape, float(k), F32)
    vals = []
    for j in range(k):
        m = jnp.max(s, axis=0, keepdims=True)
        first = jnp.min(jnp.where(s == m, order, jnp.inf), axis=0, keepdims=True)
        sel = order == first
        rank = jnp.where(sel, float(j), rank)
        s = jnp.where(sel, -jnp.inf, s)
        vals.append(m)
    return vals, rank


def _topk_values(s, k):
    vals = []
    for _ in range(k):
        m = jnp.max(s, axis=0, keepdims=True)
        s = jnp.where(s == m, -jnp.inf, s)
        vals.append(m)
    return vals


def _sorted_top16(s):
    n = TOPK
    assert s.shape[0] == n * SUBLANES
    t = [s[v * SUBLANES:(v + 1) * SUBLANES] for v in range(n)]

    def exchange(i, j, descending):
        hi, lo = jnp.maximum(t[i], t[j]), jnp.minimum(t[i], t[j])
        t[i], t[j] = (hi, lo) if descending else (lo, hi)

    k = 2
    while k <= n:
        j = k // 2
        while j >= 1:
            for i in range(n):
                if i & j == 0:
                    exchange(i, i | j, (i & k) == 0 or k == n)
            j //= 2
        k *= 2
    dist = 1
    while dist < SUBLANES:
        other = [pltpu.roll(x, SUBLANES - dist, 0) for x in t]
        t = [jnp.maximum(t[p], other[n - 1 - p]) for p in range(n)]
        j = n // 2
        while j >= 1:
            for i in range(n):
                if i & j == 0:
                    exchange(i, i | j, True)
            j //= 2
        dist *= 2
    return [x[0:1] for x in t]


def _rank_from_values(s, vals):
    rank = jnp.full(s.shape, float(len(vals)), F32)
    for j, v in enumerate(vals):
        rank = jnp.where(s == v, float(j), rank)
    return rank


_CAND_ROWS = TOPK + 7 * 8 + 8


def _cand_flat_index(tl):
    r = lax.broadcasted_iota(jnp.int32, (_CAND_ROWS, tl), 0)
    q = r - TOPK
    mid = TOPK + (q >> 3) * TOPK + (q & 7)
    last = (r - (TOPK + 7 * 8) + 8) * TOPK
    return jnp.where(r < TOPK, r, jnp.where(r < TOPK + 7 * 8, mid, last)).astype(F32)


def _route_chunk(s0, s1, v0, rank0, v1, rank1, exact):
    tl = s0.shape[1]
    sv1 = jnp.concatenate(v1, axis=0)
    sv1_lo = sv1[0:8]
    sv0_hi = jnp.concatenate(v0[8:], axis=0)
    cand = jnp.concatenate(
        [v0[0] + sv1] + [v0[a] + sv1_lo for a in range(1, 8)] + [sv0_hi + v1[0]], axis=0)
    if exact:
        _, crank = _topk_rank(cand, _cand_flat_index(tl), TOPK)
        chosen = jnp.where(crank < float(TOPK), 1.0, 0.0)
    else:
        pad = jnp.full((TOPK * SUBLANES - _CAND_ROWS, tl), -jnp.inf, F32)
        kth = _sorted_top16(jnp.concatenate([cand, pad], axis=0))[-1]
        chosen = jnp.where(cand >= kth, 1.0, 0.0)
    z = jnp.sum(chosen * jnp.exp(cand - (v0[0] + v1[0])), axis=0, keepdims=True)
    if exact:
        counts = [jnp.sum(chosen[0:TOPK], axis=0, keepdims=True)]
        counts += [jnp.sum(chosen[TOPK + 8 * (a - 1):TOPK + 8 * a], axis=0, keepdims=True)
                   for a in range(1, 8)]
        counts += [chosen[TOPK + 7 * 8 + a:TOPK + 7 * 8 + a + 1] for a in range(8)]
        nsel = jnp.zeros(s0.shape, F32)
        for a in range(TOPK):
            nsel = nsel + jnp.where(rank0 == float(a), counts[a], 0.0)
    else:
        m_lo = chosen[0:8]
        for a in range(1, 8):
            m_lo = m_lo + chosen[TOPK + 8 * (a - 1):TOPK + 8 * a]
        tail = jnp.sum(chosen[TOPK + 7 * 8:], axis=0, keepdims=True)
        sub = lax.broadcasted_iota(jnp.int32, m_lo.shape, 0)
        m_lo = m_lo + jnp.where(sub == 0, tail, 0.0)
        m_hi = jnp.sum(chosen[8:TOPK], axis=0, keepdims=True)
        thresh = jnp.full(m_lo.shape, jnp.inf, F32)
        for a in range(TOPK):
            thresh = jnp.where(m_lo == float(a + 1), v0[a], thresh)
        nsel = jnp.where(s0 >= v0[0], m_hi, 0.0)
        for b in range(8):
            nsel = nsel + jnp.where(s0 >= thresh[b:b + 1], 1.0, 0.0)
    c = jnp.exp(s0 - v0[0]) / z
    e1 = jnp.exp(s1 - v1[0])
    return nsel, c, rank1, e1, jnp.sum(chosen, axis=0, keepdims=True)


def _dup_bf16_words(x):
    hi = pltpu.bitcast(x.astype(BF16).astype(F32), jnp.uint32)
    return hi | (hi >> 16)


def _router_kernel(x_ref, g_ref, sh_ref, sc_ref, wqt_ref, keys_ref,
                   xt_ref, nf_ref, c_ref, r1_ref, e1_ref, qt_scr):
    tt, b, d = x_ref.shape
    tm = tt * b
    hn = _norm_mod(x_ref[...], g_ref[...], sh_ref[...], sc_ref[...]).reshape(tm, d)
    xt = hn.T.astype(BF16)
    xt_ref[...] = xt
    qt_scr[...] = jnp.dot(wqt_ref[...], xt, preferred_element_type=F32)

    def head(h, carry):
        def scores(lanes):
            s = []
            for p in range(2):
                row0 = pl.multiple_of((2 * h + p) * N_KEYS, N_KEYS)
                qc = qt_scr[pl.ds(row0, N_KEYS), lanes].astype(BF16)
                s.append(jnp.dot(keys_ref[2 * h + p], qc, preferred_element_type=F32))
            return s

        def emit(lanes, s, v0, rank0, v1, rank1, exact):
            nsel, c, rank1, e1, n_chosen = _route_chunk(s[0], s[1], v0, rank0, v1, rank1, exact)
            nf_ref[h, :, lanes] = _dup_bf16_words(nsel)
            c_ref[h, :, lanes] = _dup_bf16_words(c)
            r1_ref[h, :, lanes] = pltpu.bitcast(rank1.astype(BF16), jnp.uint32)
            e1_ref[h, :, lanes] = pltpu.bitcast(e1.astype(BF16), jnp.uint32)
            return n_chosen

        tied = []
        for lo in range(0, tm, LANES):
            lanes = slice(lo, lo + LANES)
            s = scores(lanes)
            v0 = _sorted_top16(s[0])
            v1 = _sorted_top16(s[1])
            n_chosen = emit(lanes, s, v0, None, v1, _rank_from_values(s[1], v1), False)
            extra = n_chosen
            for p, v in enumerate((v0, v1)):
                reach = jnp.sum(jnp.where(s[p] >= v[-1], 1.0, 0.0), axis=0, keepdims=True)
                for a in range(TOPK - 1):
                    reach = reach + jnp.where(v[a] == v[a + 1], 1.0, 0.0)
                extra = jnp.maximum(extra, reach)
            tied.append(jnp.max(extra) > float(TOPK))

        for lo, redo in zip(range(0, tm, LANES), tied):
            @pl.when(redo)
            def _(lo=lo):
                lanes = slice(lo, lo + LANES)
                s = scores(lanes)
                key_order = lax.broadcasted_iota(jnp.int32, (N_KEYS, LANES), 0).astype(F32)
                emit(lanes, s, *_topk_rank(s[0], key_order, TOPK),
                     *_topk_rank(s[1], key_order, TOPK), True)
        return carry

    lax.fori_loop(0, PEER_HEADS, head, 0)


def _router(x, g, sh, sc, wqt_bf16, keys_bf16):
    t, b, d = x.shape
    n = t * b
    tt = _time_tile(ROUTER_ROW_TILE, t, b)
    tm = tt * b
    dq = wqt_bf16.shape[0]
    assert tm % LANES == 0
    fac = jax.ShapeDtypeStruct((PEER_HEADS, N_KEYS, n), jnp.uint32)
    fac16 = jax.ShapeDtypeStruct((PEER_HEADS, N_KEYS // 2, n), jnp.uint32)
    fac_spec = pl.BlockSpec((PEER_HEADS, N_KEYS, tm), lambda i: (0, 0, i))
    fac16_spec = pl.BlockSpec((PEER_HEADS, N_KEYS // 2, tm), lambda i: (0, 0, i))
    return pl.pallas_call(
        _router_kernel,
        grid=(t // tt,),
        in_specs=[
            pl.BlockSpec((tt, b, d), lambda i: (i, 0, 0)),
            pl.BlockSpec((1, d), lambda i: (0, 0)),
            pl.BlockSpec((b, d), lambda i: (0, 0)),
            pl.BlockSpec((b, d), lambda i: (0, 0)),
            pl.BlockSpec((dq, d), lambda i: (0, 0)),
            pl.BlockSpec((2 * PEER_HEADS, N_KEYS, N_KEYS), lambda i: (0, 0, 0)),
        ],
        out_specs=[pl.BlockSpec((d, tm), lambda i: (0, i)), fac_spec, fac_spec,
                   fac16_spec, fac16_spec],
        out_shape=[jax.ShapeDtypeStruct((d, n), BF16), fac, fac, fac16, fac16],
        scratch_shapes=[pltpu.VMEM((dq, tm), F32)],
        compiler_params=_params("parallel"),
        name="peer_router",
    )(x, g, sh, sc, wqt_bf16, keys_bf16)


def _bcast_row_words(row):
    return pltpu.bitcast(jnp.broadcast_to(row, (SUBLANES, row.shape[1])), BF16)


def _peer_kernel(xt_ref, u_ref, vt_ref, nf_ref, c_ref, r1_ref, e1_ref, x_ref, g2_ref,
                 o_ref, acc_ref, act_ref, w_ref):
    tt, b, d = x_ref.shape
    tm = tt * b
    tn = u_ref.shape[0]
    groups = tn // N_KEYS
    assert 2 * groups == SUBLANES
    k = pl.program_id(1)
    n_tiles = pl.num_programs(1) - 1

    @pl.when(k == 0)
    def _():
        acc_ref[...] = jnp.zeros_like(acc_ref)
        act_ref[...] = jnp.zeros_like(act_ref)

    def gate_chunk(prev, lo):
        lanes = slice(lo, lo + LANES)
        reps = N_KEYS // BF16_ROWS
        gates = [None] * groups
        for h in range(PEER_HEADS):
            r1 = pltpu.bitcast(r1_ref[h, :, lanes], BF16)
            e1 = pltpu.bitcast(e1_ref[h, :, lanes], BF16)
            n_rows = nf_ref[h, :, lanes]
            c_rows = c_ref[h, :, lanes]
            for ii in range(groups):
                r = prev * groups + ii
                n_b = jnp.tile(_bcast_row_words(n_rows[r:r + 1]), (reps, 1))
                c_b = jnp.tile(_bcast_row_words(c_rows[r:r + 1]), (reps, 1))
                term = c_b * jnp.where(r1 < n_b, e1, 0.0)
                gates[ii] = term if gates[ii] is None else gates[ii] + term
        for ii in range(groups):
            rows = slice(ii * N_KEYS, (ii + 1) * N_KEYS)
            w_ref[rows, lanes] = gates[ii] * act_ref[prev, rows, lanes]

    def step(cur):
        prev = 1 - cur
        for lo in range(0, tm, MXU_COLS):
            cols = slice(lo, lo + MXU_COLS)
            s = jnp.dot(u_ref[...], xt_ref[:, cols], preferred_element_type=F32)
            act = (0.5 * s) * (1.0 + lax.erf(s * (0.5 ** 0.5)))
            act_ref[cur, :, cols] = act.astype(BF16)
            for sub in range(lo, lo + MXU_COLS, LANES):
                gate_chunk(prev, sub)
            acc_ref[:, cols] += jnp.dot(vt_ref[...], w_ref[:, cols],
                                        preferred_element_type=F32)

    @pl.when(k % 2 == 0)
    def _():
        step(0)

    @pl.when(k % 2 == 1)
    def _():
        step(1)

    @pl.when(k == n_tiles)
    def _():
        o = acc_ref[...].T.reshape(tt, b, d)
        o_ref[...] = x_ref[...] + g2_ref[...] * o


def _peer(xt, u_bf16, vt_bf16, layer, nf, c, r1, e1, x, g2):
    t, b, d = x.shape
    n = t * b
    ne = u_bf16.shape[1]
    tt = _time_tile(PEER_ROW_TILE, t, b)
    tm = tt * b
    tn = EXPERT_TILE
    n_tiles = ne // tn
    groups = tn // N_KEYS
    fac_spec = pl.BlockSpec(
        (PEER_HEADS, SUBLANES, tm),
        lambda i, j: (0, jnp.clip(j - 1, 0, n_tiles - 1) * groups // SUBLANES, i))
    fac16_spec = pl.BlockSpec((PEER_HEADS, N_KEYS // 2, tm), lambda i, j: (0, 0, i))
    return pl.pallas_call(
        _peer_kernel,
        grid=(n // tm, n_tiles + 1),
        in_specs=[
            pl.BlockSpec((d, tm), lambda i, j: (0, i)),
            pl.BlockSpec((None, tn, d), lambda i, j: (layer, jnp.minimum(j, n_tiles - 1), 0)),
            pl.BlockSpec((None, d, tn),
                         lambda i, j: (layer, 0, jnp.clip(j - 1, 0, n_tiles - 1))),
            fac_spec, fac_spec, fac16_spec, fac16_spec,
            pl.BlockSpec((tt, b, d), lambda i, j: (i, 0, 0)),
            pl.BlockSpec((b, d), lambda i, j: (0, 0)),
        ],
        out_specs=pl.BlockSpec((tt, b, d), lambda i, j: (i, 0, 0)),
        out_shape=jax.ShapeDtypeStruct((t, b, d), F32),
        scratch_shapes=[
            pltpu.VMEM((d, tm), F32),
            pltpu.VMEM((2, tn, tm), BF16),
            pltpu.VMEM((tn, tm), BF16),
        ],
        compiler_params=_params("parallel", "arbitrary"),
        name="peer_experts",
    )(xt, u_bf16, vt_bf16, nf, c, r1, e1, x, g2)


def _final_norm_kernel(x_ref, g_ref, o_ref):
    x = x_ref[...]
    var = jnp.mean(x * x, axis=-1, keepdims=True)
    o_ref[...] = (x * lax.rsqrt(var + EPS)) * g_ref[...]


def _final_norm(x, g):
    t, b, d = x.shape
    tt = _time_tile(ROW_TILE, t, b)
    return pl.pallas_call(
        _final_norm_kernel,
        grid=(t // tt,),
        in_specs=[pl.BlockSpec((tt, b, d), lambda i: (i, 0, 0)),
                  pl.BlockSpec((1, d), lambda i: (0, 0))],
        out_specs=pl.BlockSpec((tt, b, d), lambda i: (i, 0, 0)),
        out_shape=jax.ShapeDtypeStruct((t, b, d), F32),
        compiler_params=_params("parallel"),
        name="final_norm",
    )(x, g)


def _block_diag(w):
    nb, bs, _ = w.shape
    eye = jnp.eye(nb, dtype=w.dtype)
    return (eye[:, None, :, None] * w[:, :, None, :]).reshape(nb * bs, nb * bs)


def _layer(x, mod, states, is_prompt, lw):
    sh1, sc1, g1, sh2, sc2, g2 = mod
    bufa, h0, bufb = states
    proj = _inproj(x, lw["norm1"], sh1, sc1, lw["w_in"], lw["layer"])
    x, na, nh, nb = _mixer(is_prompt, proj, x, bufa, h0, bufb, g1,
                           lw["conv_a_w"], lw["conv_a_b"], lw["rg_w_a"], lw["rg_b_a"],
                           lw["rg_w_x"], lw["rg_b_x"], lw["rg_lambda"], lw["conv_b_w"], lw["w_out"])
    xt, nf, c, r1, e1 = _router(x, lw["norm2"], sh2, sc2, lw["w_qt"], lw["keys"])
    x = _peer(xt, lw["u"], lw["vt"], lw["layer"], nf, c, r1, e1, x, g2)
    return x, na, nh, nb


def kernel(x_prompt, x_sample, c_prompt, c_sample, state_conv_a, state_h, state_conv_b, w_ada, b_ada, norm1, norm2, w_in, conv_a_w, conv_a_b, rg_w_a, rg_b_a, rg_w_x, rg_b_x, rg_lambda, conv_b_w, w_out, peer_w_q, peer_sub_keys, peer_u, peer_v, final_norm):
    depth = w_ada.shape[0]
    bp, _, d = x_prompt.shape
    bs = x_sample.shape[0]
    wa_hist = conv_a_w.shape[1] - 1
    wb_hist = conv_b_w.shape[1] - 1

    mods = _ada(jnp.concatenate([c_prompt, c_sample], axis=0), w_ada, b_ada)
    mods = mods.reshape(depth, bp + bs, N_MOD, d)

    xp = jnp.transpose(x_prompt, (1, 0, 2))
    xs = jnp.transpose(x_sample, (1, 0, 2))
    zeros_p = (jnp.zeros((wa_hist, bp, d), F32), jnp.zeros((bp, d), F32),
               jnp.zeros((wb_hist, bp, d), F32))
    outs_p, outs_s = [], []
    w_in_bf16 = w_in.astype(BF16)
    u_bf16 = peer_u.astype(BF16)
    vt_bf16 = jnp.swapaxes(peer_v, 1, 2).astype(BF16)
    for l in range(depth):
        lw = {
            "layer": l,
            "norm1": norm1[l].reshape(1, d), "norm2": norm2[l].reshape(1, d),
            "w_in": w_in_bf16,
            "conv_a_w": conv_a_w[l], "conv_a_b": conv_a_b[l],
            "rg_w_a": _block_diag(rg_w_a[l]).astype(BF16), "rg_b_a": rg_b_a[l],
            "rg_w_x": _block_diag(rg_w_x[l]).astype(BF16), "rg_b_x": rg_b_x[l],
            "rg_lambda": rg_lambda[l], "conv_b_w": conv_b_w[l],
            "w_out": w_out[l].astype(BF16),
            "w_qt": peer_w_q[l].T.astype(BF16),
            "keys": peer_sub_keys[l].reshape(2 * PEER_HEADS, N_KEYS, -1).astype(BF16),
            "u": u_bf16,
            "vt": vt_bf16,
        }
        mod_p = [mods[l, :bp, k] for k in range(N_MOD)]
        mod_s = [mods[l, bp:, k] for k in range(N_MOD)]
        xp, na, nh, nb = _layer(xp, mod_p, zeros_p, True, lw)
        outs_p.append((na, nh, nb))
        states_s = (jnp.transpose(state_conv_a[l], (1, 0, 2)), state_h[l],
                    jnp.transpose(state_conv_b[l], (1, 0, 2)))
        xs, na, nh, nb = _layer(xs, mod_s, states_s, False, lw)
        outs_s.append((na, nh, nb))

    fn = final_norm.reshape(1, d)
    y_prompt = jnp.transpose(_final_norm(xp, fn), (1, 0, 2))
    y_sample = jnp.transpose(_final_norm(xs, fn), (1, 0, 2))

    def stack(outs):
        ca = jnp.stack([jnp.transpose(o[0], (1, 0, 2)) for o in outs], 0)
        hh = jnp.stack([o[1] for o in outs], 0)
        cb = jnp.stack([jnp.transpose(o[2], (1, 0, 2)) for o in outs], 0)
        return ca, hh, cb

    ca_p, hh_p, cb_p = stack(outs_p)
    ca_s, hh_s, cb_s = stack(outs_s)
    return (y_prompt, y_sample, ca_p, hh_p, cb_p, ca_s, hh_s, cb_s)
```
